```python
import math
import jax, jax.numpy as jnp
from jax import lax
import numpy as np

D_MODEL = 1024
BATCH = 4
SEQ = 8192
DEPTH = 4
DEC_BATCH = 4
DEC_SEQ = 4096
PAST_LEN = 128

ATTN_WIDTH = D_MODEL // 2
CONV_WIDTH = D_MODEL - ATTN_WIDTH
HEAD_DIM = 64
V_DIM = 2 * HEAD_DIM
N_ATTN_HEADS = ATTN_WIDTH // V_DIM
N_CONV_GROUPS = 8
CONV_K = 3
D_FF = 4 * D_MODEL
NUM_BUCKETS = 32
MAX_DISTANCE = 128
Q_BLOCK = 128
QK_COLS = N_ATTN_HEADS * 2 * HEAD_DIM
IN_COLS = 3 * ATTN_WIDTH + 3 * CONV_WIDTH
LN_EPS = 1e-5
DEEPNORM_ALPHA = (2.0 * DEPTH) ** 0.25
DEEPNORM_BETA = (8.0 * DEPTH) ** -0.25

kernel_name = "hybrid_diffattn_shortconv_deepnorm_encoder"


def layer_norm(x, g, b):
    xf = x.astype(jnp.float32)
    mu = jnp.mean(xf, axis=-1, keepdims=True)
    var = jnp.mean(jnp.square(xf - mu), axis=-1, keepdims=True)
    y = (xf - mu) * lax.rsqrt(var + LN_EPS) * g.astype(jnp.float32) + b.astype(jnp.float32)
    return y.astype(x.dtype)


def t5_bucket(rel):
    half = NUM_BUCKETS // 2
    max_exact = half // 2
    ret = jnp.where(rel > 0, half, 0)
    n = jnp.abs(rel)
    nf = jnp.maximum(n, 1).astype(jnp.float32)
    large = max_exact + (jnp.log(nf / max_exact) / math.log(MAX_DISTANCE / max_exact)
                         * (half - max_exact)).astype(jnp.int32)
    large = jnp.minimum(large, half - 1)
    return ret + jnp.where(n < max_exact, n, large)


def diff_attention(q, k, v, lam, rel_bias):
    b, s = q.shape[0], q.shape[1]
    nb = s // Q_BLOCK
    scale = HEAD_DIM ** -0.5
    q1, q2 = q[..., :HEAD_DIM], q[..., HEAD_DIM:]
    k1, k2 = k[..., :HEAD_DIM], k[..., HEAD_DIM:]
    kpos = jnp.arange(s, dtype=jnp.int32)

    def block(args):
        i, q1b, q2b = args
        qpos = i * Q_BLOCK + jnp.arange(Q_BLOCK, dtype=jnp.int32)
        bucket = t5_bucket(kpos[None, :] - qpos[:, None])
        bias = jnp.transpose(rel_bias[bucket], (2, 0, 1))
        p1 = jax.nn.softmax(jnp.einsum('bqhd,bkhd->bhqk', q1b, k1) * scale + bias, axis=-1)
        p2 = jax.nn.softmax(jnp.einsum('bqhd,bkhd->bhqk', q2b, k2) * scale + bias, axis=-1)
        return jnp.einsum('bhqk,bkhe->bqhe', p1 - lam * p2, v)

    def to_blocks(t):
        return jnp.moveaxis(t.reshape(b, nb, Q_BLOCK, *t.shape[2:]), 1, 0)

    out = lax.map(block, (jnp.arange(nb, dtype=jnp.int32), to_blocks(q1), to_blocks(q2)))
    return jnp.moveaxis(out, 0, 1).reshape(b, s, N_ATTN_HEADS, V_DIM)


def short_conv(u, w, bias):
    up = jnp.pad(u, ((0, 0), (1, 1), (0, 0)))
    return w[0] * up[:, :-2] + w[1] * up[:, 1:-1] + w[2] * up[:, 2:] + bias


def encoder_layer(x, l, w_in, w_out, conv_w, conv_b, lambda_q1, lambda_k1, lambda_q2,
                  lambda_k2, subln_g, rel_bias, ln1_g, ln1_b, w_mlp1, w_mlp2, ln2_g, ln2_b):
    b, s, _ = x.shape
    proj = jnp.einsum('bsd,dn->bsn', x, w_in[l])
    q, k, v, gb, gc, hc = jnp.split(
        proj, [QK_COLS, 2 * QK_COLS, 2 * QK_COLS + ATTN_WIDTH,
               2 * QK_COLS + ATTN_WIDTH + CONV_WIDTH,
               2 * QK_COLS + ATTN_WIDTH + 2 * CONV_WIDTH], axis=-1)
    f32 = jnp.float32
    q = q.reshape(b, s, N_ATTN_HEADS, 2 * HEAD_DIM).astype(f32)
    k = k.reshape(b, s, N_ATTN_HEADS, 2 * HEAD_DIM).astype(f32)
    v = v.reshape(b, s, N_ATTN_HEADS, V_DIM).astype(f32)
    lambda_init = 0.8 - 0.6 * math.exp(-0.3 * l)
    lam = (jnp.exp(jnp.sum(lambda_q1[l].astype(f32) * lambda_k1[l].astype(f32)))
           - jnp.exp(jnp.sum(lambda_q2[l].astype(f32) * lambda_k2[l].astype(f32)))
           + lambda_init)
    a = diff_attention(q, k, v, lam, rel_bias.astype(f32))
    a = a * lax.rsqrt(jnp.mean(jnp.square(a), axis=-1, keepdims=True) + LN_EPS)
    a = a * subln_g[l].astype(f32) * (1.0 - lambda_init)
    attn_out = a.reshape(b, s, ATTN_WIDTH).astype(x.dtype)
    conv_out = gb * short_conv(gc * hc, conv_w[l], conv_b[l])
    mix = jnp.einsum('bsn,nd->bsd', jnp.concatenate([attn_out, conv_out], axis=-1), w_out[l])
    x = layer_norm(DEEPNORM_ALPHA * x + mix, ln1_g[l], ln1_b[l])
    hdn = jnp.square(jax.nn.relu(jnp.einsum('bsd,df->bsf', x, w_mlp1[l])))
    ffn = jnp.einsum('bsf,fd->bsd', hdn, w_mlp2[l])
    return layer_norm(DEEPNORM_ALPHA * x + ffn, ln2_g[l], ln2_b[l])


def setup_inputs(seed: int = 0) -> dict:
    key = jax.random.key(seed)
    ks = jax.random.split(key, 20)
    nrm = jax.random.normal
    f32 = jnp.float32
    x_prompt = nrm(ks[0], (BATCH, SEQ, D_MODEL), f32)
    x_sample = nrm(ks[1], (DEC_BATCH, DEC_SEQ, D_MODEL), f32)
    col_scale = jnp.concatenate([
        jnp.ones((2 * QK_COLS,), f32),
        jnp.full((ATTN_WIDTH,), DEEPNORM_BETA, f32),
        jnp.ones((2 * CONV_WIDTH,), f32),
        jnp.full((CONV_WIDTH,), DEEPNORM_BETA, f32)])
    w_in = nrm(ks[2], (DEPTH, D_MODEL, IN_COLS), f32) * (D_MODEL ** -0.5) * col_scale
    w_out = nrm(ks[3], (DEPTH, D_MODEL, D_MODEL), f32) * (D_MODEL ** -0.5) * DEEPNORM_BETA
    conv_w = nrm(ks[4], (DEPTH, CONV_K, CONV_WIDTH), f32) * (CONV_K ** -0.5)
    conv_b = nrm(ks[5], (DEPTH, CONV_WIDTH), f32) * 0.02
    lambda_q1 = nrm(ks[6], (DEPTH, HEAD_DIM), f32) * 0.1
    lambda_k1 = nrm(ks[7], (DEPTH, HEAD_DIM), f32) * 0.1
    lambda_q2 = nrm(ks[8], (DEPTH, HEAD_DIM), f32) * 0.1
    lambda_k2 = nrm(ks[9], (DEPTH, HEAD_DIM), f32) * 0.1
    subln_g = 1.0 + 0.02 * nrm(ks[10], (DEPTH, V_DIM), f32)
    rel_bias = nrm(ks[11], (NUM_BUCKETS, N_ATTN_HEADS), f32) * 0.5
    ln1_g = 1.0 + 0.02 * nrm(ks[12], (DEPTH, D_MODEL), f32)
    ln1_b = 0.02 * nrm(ks[13], (DEPTH, D_MODEL), f32)
    w_mlp1 = nrm(ks[14], (DEPTH, D_MODEL, D_FF), f32) * (D_MODEL ** -0.5) * DEEPNORM_BETA
    w_mlp2 = nrm(ks[15], (DEPTH, D_FF, D_MODEL), f32) * (D_FF ** -0.5) * DEEPNORM_BETA
    ln2_g = 1.0 + 0.02 * nrm(ks[16], (DEPTH, D_MODEL), f32)
    ln2_b = 0.02 * nrm(ks[17], (DEPTH, D_MODEL), f32)
    return {"x_prompt": x_prompt, "x_sample": x_sample, "w_in": w_in, "w_out": w_out,
            "conv_w": conv_w, "conv_b": conv_b, "lambda_q1": lambda_q1, "lambda_k1": lambda_k1,
            "lambda_q2": lambda_q2, "lambda_k2": lambda_k2, "subln_g": subln_g,
            "rel_bias": rel_bias, "ln1_g": ln1_g, "ln1_b": ln1_b, "w_mlp1": w_mlp1,
            "w_mlp2": w_mlp2, "ln2_g": ln2_g, "ln2_b": ln2_b}


def reference(x_prompt, x_sample, w_in, w_out, conv_w, conv_b, lambda_q1, lambda_k1,
              lambda_q2, lambda_k2, subln_g, rel_bias, ln1_g, ln1_b, w_mlp1, w_mlp2,
              ln2_g, ln2_b):
    y_prompt = x_prompt
    y_sample = x_sample
    for l in range(DEPTH):
        y_prompt = encoder_layer(y_prompt, l, w_in, w_out, conv_w, conv_b, lambda_q1, lambda_k1,
                                 lambda_q2, lambda_k2, subln_g, rel_bias, ln1_g, ln1_b,
                                 w_mlp1, w_mlp2, ln2_g, ln2_b)
        y_sample = encoder_layer(y_sample, l, w_in, w_out, conv_w, conv_b, lambda_q1, lambda_k1,
                                 lambda_q2, lambda_k2, subln_g, rel_bias, ln1_g, ln1_b,
                                 w_mlp1, w_mlp2, ln2_g, ln2_b)
    return (y_prompt, y_sample)
```

```python
import functools
import math

import numpy as np
import jax
import jax.numpy as jnp
from jax import lax
from jax.experimental import pallas as pl
from jax.experimental.pallas import tpu as pltpu

D_MODEL = 1024
HEAD_DIM = 64
V_DIM = 2 * HEAD_DIM
N_HEADS = 4
ATTN_WIDTH = N_HEADS * V_DIM
CONV_WIDTH = D_MODEL - ATTN_WIDTH
D_FF = 4 * D_MODEL
NUM_BUCKETS = 32
MAX_DISTANCE = 128
LN_EPS = 1e-5
HALO_ROWS = 8
SEQ_TILE = 512
FF_CHUNK = 1024
VMEM_LIMIT_BYTES = 56 * 1024 * 1024
NEG_BIG = -1e30

F32 = jnp.float32
BF16 = jnp.bfloat16


def _t5_bucket(rel):
    half = NUM_BUCKETS // 2
    max_exact = half // 2
    ret = jnp.where(rel > 0, half, 0)
    n = jnp.abs(rel)
    nf = jnp.maximum(n, 1).astype(F32)
    large = max_exact + (jnp.log(nf / max_exact) / math.log(MAX_DISTANCE / max_exact)
                         * (half - max_exact)).astype(jnp.int32)
    large = jnp.minimum(large, half - 1)
    return ret + jnp.where(n < max_exact, n, large)


def _bias_tables(rel_bias, tile):
    assert tile >= MAX_DISTANCE
    rel = jnp.arange(-2 * tile, 2 * tile, dtype=jnp.int32)
    tab = rel_bias.astype(F32)[_t5_bucket(rel)]
    kk = jnp.arange(tile, dtype=jnp.int32)[:, None]
    qq = jnp.arange(tile, dtype=jnp.int32)[None, :]
    d = jnp.arange(-1, 2, dtype=jnp.int32)[:, None, None]
    idx = d * tile + kk - qq + 2 * tile
    near = jnp.transpose(tab[idx], (3, 0, 1, 2))
    far = jnp.stack([tab[0], tab[-1]])
    return near, far


def _proj_kernel(x_ref, wk_ref, wqvt_ref, wc_ref, k_ref, qt_ref, vt_ref, gb_ref, u_ref):
    xb = x_ref[...].astype(BF16)
    k = jnp.dot(xb, wk_ref[...], preferred_element_type=F32)
    qv_t = lax.dot_general(wqvt_ref[...], xb, (((1,), (1,)), ((), ())),
                           preferred_element_type=F32)
    c = jnp.dot(xb, wc_ref[...], preferred_element_type=F32)
    scale = HEAD_DIM ** -0.5
    for h in range(N_HEADS):
        k_ref[h] = k[:, h * V_DIM:(h + 1) * V_DIM].astype(BF16)
        qt_ref[h] = (qv_t[h * V_DIM:(h + 1) * V_DIM, :] * scale).astype(BF16)
        vt_ref[h] = qv_t[ATTN_WIDTH + h * V_DIM:ATTN_WIDTH + (h + 1) * V_DIM, :].astype(BF16)
    gb_ref[...] = c[:, :CONV_WIDTH]
    u_ref[...] = c[:, CONV_WIDTH:2 * CONV_WIDTH] * c[:, 2 * CONV_WIDTH:]


def _project(x, wk, wqvt, wc, tile):
    b, s, _ = x.shape
    nt = s // tile
    const = lambda shape: pl.BlockSpec(shape, lambda bi, ti: (0,) * len(shape),
                                       pipeline_mode=pl.Buffered(1))
    head_major = lambda shape: pl.BlockSpec((None, N_HEADS, None) + shape,
                                            lambda bi, ti: (bi, 0, ti, 0, 0))
    rows = lambda width: pl.BlockSpec((None, tile, width), lambda bi, ti: (bi, ti, 0))
    return pl.pallas_call(
        _proj_kernel,
        grid=(b, nt),
        in_specs=[rows(D_MODEL), const(wk.shape), const(wqvt.shape), const(wc.shape)],
        out_specs=[head_major((tile, V_DIM)), head_major((V_DIM, tile)),
                   head_major((V_DIM, tile)), rows(CONV_WIDTH), rows(CONV_WIDTH)],
        out_shape=[jax.ShapeDtypeStruct((b, N_HEADS, nt, tile, V_DIM), BF16),
                   jax.ShapeDtypeStruct((b, N_HEADS, nt, V_DIM, tile), BF16),
                   jax.ShapeDtypeStruct((b, N_HEADS, nt, V_DIM, tile), BF16),
                   jax.ShapeDtypeStruct((b, s, CONV_WIDTH), F32),
                   jax.ShapeDtypeStruct((b, s, CONV_WIDTH), F32)],
        compiler_params=pltpu.CompilerParams(
            dimension_semantics=("parallel", "parallel"),
            vmem_limit_bytes=VMEM_LIMIT_BYTES),
        name="in_proj",
    )(x, wk, wqvt, wc)


def _attn_kernel(far_ref, linit_ref, qt_ref, k_ref, vt_ref, near_ref, lamv_ref, g_ref,
                 o_ref, qz_ref, m_ref, l_ref, acc_ref, *, n_tiles):
    h = pl.program_id(1)
    qi = pl.program_id(2)

    qt = qt_ref[...].astype(F32)
    row = lax.broadcasted_iota(jnp.int32, qt.shape, 0)
    qz_ref[0] = jnp.where(row < HEAD_DIM, qt, 0.0).astype(BF16)
    qz_ref[1] = jnp.where(row >= HEAD_DIM, qt, 0.0).astype(BF16)
    m_ref[...] = jnp.full(m_ref.shape, NEG_BIG, F32)
    l_ref[...] = jnp.zeros(l_ref.shape, F32)
    acc_ref[...] = jnp.zeros(acc_ref.shape, F32)

    def update(ki, bias_t, c):
        kt = k_ref[ki]
        vt = vt_ref[ki]
        for mp in range(2):
            s = jnp.dot(kt, qz_ref[mp], preferred_element_type=F32)
            if bias_t is not None:
                s = s + bias_t
            m_prev = m_ref[mp]
            m_new = jnp.maximum(m_prev, jnp.max(s, axis=0, keepdims=True) + c)
            alpha = jnp.exp(m_prev - m_new)
            p = jnp.exp(s - (m_new - c))
            l_ref[mp] = alpha * l_ref[mp] + jnp.sum(p, axis=0, keepdims=True)
            acc_ref[mp] = alpha * acc_ref[mp] + jnp.dot(
                vt, p.astype(BF16), preferred_element_type=F32)
            m_ref[mp] = m_new

    def far_before(ki, carry):
        update(ki, None, far_ref[0, h])
        return carry

    def near(ki, carry):
        update(ki, near_ref[ki - qi + 1], 0.0)
        return carry

    def far_after(ki, carry):
        update(ki, None, far_ref[1, h])
        return carry

    lax.fori_loop(0, jnp.maximum(qi - 1, 0), far_before, 0)
    lax.fori_loop(jnp.maximum(qi - 1, 0), jnp.minimum(qi + 2, n_tiles), near, 0)
    lax.fori_loop(qi + 2, n_tiles, far_after, 0)

    lamv = lamv_ref[...]
    lam = (jnp.exp(jnp.sum(lamv[0:1] * lamv[1:2], axis=1, keepdims=True))
           - jnp.exp(jnp.sum(lamv[2:3] * lamv[3:4], axis=1, keepdims=True))
           + linit_ref[0])
    o = acc_ref[0] / l_ref[0] - lam * (acc_ref[1] / l_ref[1])
    o = o * lax.rsqrt(jnp.mean(jnp.square(o), axis=0, keepdims=True) + LN_EPS)
    o = o * g_ref[...] * (1.0 - linit_ref[0])
    o_ref[...] = o.T.astype(o_ref.dtype)


def _attention(far, linit, q_t, k, v_t, near, lamv, g_col, seq_len):
    b, _, n_tiles, _, tile = q_t.shape
    smem = pl.BlockSpec(memory_space=pltpu.SMEM)
    per_head_seq = lambda shape: pl.BlockSpec((None, None, n_tiles) + shape,
                                              lambda bi, hi, qi: (bi, hi, 0, 0, 0))
    return pl.pallas_call(
        functools.partial(_attn_kernel, n_tiles=n_tiles),
        grid=(b, N_HEADS, n_tiles),
        in_specs=[smem, smem,
                  pl.BlockSpec((None, None, None, V_DIM, tile),
                               lambda bi, hi, qi: (bi, hi, qi, 0, 0)),
                  per_head_seq((tile, V_DIM)),
                  per_head_seq((V_DIM, tile)),
                  pl.BlockSpec((None, 3, tile, tile), lambda bi, hi, qi: (hi, 0, 0, 0)),
                  pl.BlockSpec(lamv.shape, lambda bi, hi, qi: (0, 0)),
                  pl.BlockSpec(g_col.shape, lambda bi, hi, qi: (0, 0))],
        out_specs=pl.BlockSpec((None, tile, V_DIM), lambda bi, hi, qi: (bi, qi, hi)),
        out_shape=jax.ShapeDtypeStruct((b, seq_len, ATTN_WIDTH), BF16),
        scratch_shapes=[pltpu.VMEM((2, V_DIM, tile), BF16),
                        pltpu.VMEM((2, 1, tile), F32),
                        pltpu.VMEM((2, 1, tile), F32),
                        pltpu.VMEM((2, V_DIM, tile), F32)],
        compiler_params=pltpu.CompilerParams(
            dimension_semantics=("parallel", "parallel", "arbitrary"),
            vmem_limit_bytes=VMEM_LIMIT_BYTES),
        name="diff_attn",
    )(far, linit, q_t, k, v_t, near, lamv, g_col)


def _layer_norm(y, g, b):
    mu = jnp.mean(y, axis=-1, keepdims=True)
    yc = y - mu
    var = jnp.mean(jnp.square(yc), axis=-1, keepdims=True)
    return yc * lax.rsqrt(var + LN_EPS) * g + b


def _mix_mlp_kernel(attn_ref, gb_ref, u_ref, uprev_ref, unext_ref, x_ref, wout_ref, cw_ref,
                    cb_ref, ln1g_ref, ln1b_ref, w1_ref, w2_ref, ln2g_ref, ln2b_ref, o_ref,
                    *, alpha, n_tiles):
    ti = pl.program_id(1)
    u = u_ref[...]
    tile = u.shape[0]
    prev_row = jnp.where(ti > 0, uprev_ref[HALO_ROWS - 1:HALO_ROWS, :], 0.0)
    next_row = jnp.where(ti < n_tiles - 1, unext_ref[0:1, :], 0.0)
    row = lax.broadcasted_iota(jnp.int32, u.shape, 0)
    u_prev = jnp.where(row == 0, prev_row, pltpu.roll(u, 1, axis=0))
    u_next = jnp.where(row == tile - 1, next_row, pltpu.roll(u, tile - 1, axis=0))
    cw = cw_ref[...]
    conv = cw[0:1] * u_prev + cw[1:2] * u + cw[2:3] * u_next + cb_ref[...]
    conv_out = (gb_ref[...] * conv).astype(BF16)
    mix = (jnp.dot(attn_ref[...], wout_ref[:ATTN_WIDTH, :], preferred_element_type=F32)
           + jnp.dot(conv_out, wout_ref[ATTN_WIDTH:, :], preferred_element_type=F32))
    x1 = _layer_norm(alpha * x_ref[...] + mix, ln1g_ref[...], ln1b_ref[...])
    x1b = x1.astype(BF16)
    ffn = jnp.zeros_like(x1)
    for c0 in range(0, D_FF, FF_CHUNK):
        hid = jnp.dot(x1b, w1_ref[:, c0:c0 + FF_CHUNK], preferred_element_type=F32)
        hid = jnp.square(jnp.maximum(hid, 0.0)).astype(BF16)
        ffn = ffn + jnp.dot(hid, w2_ref[c0:c0 + FF_CHUNK, :], preferred_element_type=F32)
    o_ref[...] = _layer_norm(alpha * x1 + ffn, ln2g_ref[...], ln2b_ref[...])


def _mix_mlp(attn, gb, u, x, wout, cw, cb, ln1g, ln1b, w1, w2, ln2g, ln2b, tile, alpha):
    b, s, _ = x.shape
    nt = s // tile
    halo_per_tile = tile // HALO_ROWS
    n_halo = s // HALO_ROWS
    const = lambda arr: pl.BlockSpec(arr.shape, lambda bi, ti: (0,) * arr.ndim,
                                     pipeline_mode=pl.Buffered(1))
    rows = lambda width: pl.BlockSpec((None, tile, width), lambda bi, ti: (bi, ti, 0))
    return pl.pallas_call(
        functools.partial(_mix_mlp_kernel, alpha=alpha, n_tiles=nt),
        grid=(b, nt),
        in_specs=[rows(ATTN_WIDTH), rows(CONV_WIDTH), rows(CONV_WIDTH),
                  pl.BlockSpec((None, HALO_ROWS, CONV_WIDTH),
                               lambda bi, ti: (bi, jnp.maximum(ti * halo_per_tile - 1, 0), 0)),
                  pl.BlockSpec((None, HALO_ROWS, CONV_WIDTH),
                               lambda bi, ti: (bi, jnp.minimum((ti + 1) * halo_per_tile,
                                                               n_halo - 1), 0)),
                  rows(D_MODEL), const(wout), const(cw), const(cb), const(ln1g), const(ln1b),
                  const(w1), const(w2), const(ln2g), const(ln2b)],
        out_specs=rows(D_MODEL),
        out_shape=jax.ShapeDtypeStruct((b, s, D_MODEL), F32),
        compiler_params=pltpu.CompilerParams(
            dimension_semantics=("parallel", "parallel"),
            vmem_limit_bytes=VMEM_LIMIT_BYTES),
        name="mix_mlp",
    )(attn, gb, u, u, u, x, wout, cw, cb, ln1g, ln1b, w1, w2, ln2g, ln2b)


def _encoder_layer(x, lw, near, far, tile, alpha):
    s = x.shape[1]
    k, q_t, v_t, gb, u = _project(x, lw["wk"], lw["wqvt"], lw["wc"], tile)
    attn = _attention(far, lw["linit"], q_t, k, v_t, near, lw["lamv"], lw["g_col"], s)
    return _mix_mlp(attn, gb, u, x, lw["wout"], lw["cw"], lw["cb"], lw["ln1g"], lw["ln1b"],
                    lw["w1"], lw["w2"], lw["ln2g"], lw["ln2b"], tile, alpha)


def kernel(x_prompt, x_sample, w_in, w_out, conv_w, conv_b, lambda_q1, lambda_k1, lambda_q2,
           lambda_k2, subln_g, rel_bias, ln1_g, ln1_b, w_mlp1, w_mlp2, ln2_g, ln2_b):
    depth = w_in.shape[0]
    alpha = (2.0 * depth) ** 0.25
    tile = SEQ_TILE
    for x in (x_prompt, x_sample):
        assert x.shape[1] % tile == 0 and x.shape[2] == D_MODEL
    near, far = _bias_tables(rel_bias, tile)

    qk = 2 * ATTN_WIDTH
    layers = []
    for l in range(depth):
        wl = w_in[l]
        linit = 0.8 - 0.6 * math.exp(-0.3 * l)
        layers.append(dict(
            wk=wl[:, ATTN_WIDTH:qk].astype(BF16),
            wqvt=jnp.concatenate([wl[:, :ATTN_WIDTH], wl[:, qk:qk + ATTN_WIDTH]],
                                 axis=1).T.astype(BF16),
            wc=wl[:, qk + ATTN_WIDTH:].astype(BF16),
            linit=jnp.full((1,), linit, F32),
            lamv=jnp.stack([lambda_q1[l], lambda_k1[l], lambda_q2[l], lambda_k2[l]]).astype(F32),
            g_col=subln_g[l].astype(F32).reshape(V_DIM, 1),
            wout=w_out[l].astype(BF16),
            cw=conv_w[l].astype(F32), cb=conv_b[l].astype(F32).reshape(1, CONV_WIDTH),
            ln1g=ln1_g[l].astype(F32).reshape(1, D_MODEL),
            ln1b=ln1_b[l].astype(F32).reshape(1, D_MODEL),
            w1=w_mlp1[l].astype(BF16), w2=w_mlp2[l].astype(BF16),
            ln2g=ln2_g[l].astype(F32).reshape(1, D_MODEL),
            ln2b=ln2_b[l].astype(F32).reshape(1, D_MODEL)))

    y_prompt, y_sample = x_prompt, x_sample
    for lw in layers:
        y_prompt = _encoder_layer(y_prompt, lw, near, far, tile, alpha)
        y_sample = _encoder_layer(y_sample, lw, near, far, tile, alpha)
    return (y_prompt, y_sample)
```

```python
import functools
import math

import jax
import jax.numpy as jnp
from jax import lax
from jax.experimental import pallas as pl
from jax.experimental.pallas import tpu as pltpu

D_MODEL = 1024
HEAD_DIM = 64
V_DIM = 2 * HEAD_DIM
BF16_SUBLANES = 16
V_ROWS = V_DIM + BF16_SUBLANES
N_HEADS = 4
ATTN_WIDTH = N_HEADS * V_DIM
CONV_WIDTH = D_MODEL - ATTN_WIDTH
D_FF = 4 * D_MODEL
NUM_BUCKETS = 32
MAX_DISTANCE = 128
LN_EPS = 1e-5
LOG2_E = math.log2(math.e)
HALO_ROWS = 8
SEQ_TILE = 512
BIAS_REACH = 2
KEY_TILES_PER_TRIP = 2
FF_CHUNK = 1024
VMEM_LIMIT_BYTES = 56 * 1024 * 1024
NEG_BIG = -1e30

F32 = jnp.float32
BF16 = jnp.bfloat16


def _t5_bucket(rel):
    half = NUM_BUCKETS // 2
    max_exact = half // 2
    ret = jnp.where(rel > 0, half, 0)
    n = jnp.abs(rel)
    nf = jnp.maximum(n, 1).astype(F32)
    large = max_exact + (jnp.log(nf / max_exact) / math.log(MAX_DISTANCE / max_exact)
                         * (half - max_exact)).astype(jnp.int32)
    large = jnp.minimum(large, half - 1)
    return ret + jnp.where(n < max_exact, n, large)


def _toeplitz_kernel(u_ref, o_ref):
    tile = o_ref.shape[0]
    x = jnp.broadcast_to(u_ref[...], (tile, 2 * tile))
    o_ref[...] = pltpu.roll(x, tile + 1, 1, stride=1, stride_axis=0)[:, :tile]


def _bias_tables(rel_bias, tile):
    assert tile >= MAX_DISTANCE
    n_d = 2 * BIAS_REACH + 1
    span = (BIAS_REACH + 1) * tile
    rel = jnp.arange(-span, span, dtype=jnp.int32)
    tab = rel_bias.astype(F32)[_t5_bucket(rel)].T * LOG2_E
    i = jnp.arange(2 * tile, dtype=jnp.int32)[None, :]
    d = jnp.arange(-BIAS_REACH, BIAS_REACH + 1, dtype=jnp.int32)[:, None]
    u = tab[:, d * tile + tile - 1 - i + span][:, :, None, :]
    return pl.pallas_call(
        _toeplitz_kernel,
        grid=(N_HEADS, n_d),
        in_specs=[pl.BlockSpec((None, None, 1, 2 * tile), lambda h, dd: (h, dd, 0, 0))],
        out_specs=pl.BlockSpec((None, None, tile, tile), lambda h, dd: (h, dd, 0, 0)),
        out_shape=jax.ShapeDtypeStruct((N_HEADS, n_d, tile, tile), F32),
        name="bias_toeplitz",
    )(u)


def _proj_kernel(x_ref, wk_ref, wqvt_ref, wc_ref, k_ref, qt_ref, vt_ref, gb_ref, u_ref):
    xb = x_ref[...].astype(BF16)
    k = jnp.dot(xb, wk_ref[...], preferred_element_type=F32)
    qv_t = lax.dot_general(wqvt_ref[...], xb, (((1,), (1,)), ((), ())),
                           preferred_element_type=F32)
    c = jnp.dot(xb, wc_ref[...], preferred_element_type=F32)
    scale = HEAD_DIM ** -0.5 * LOG2_E
    pad_row = lax.broadcasted_iota(jnp.int32, (V_ROWS - V_DIM, xb.shape[0]), 0)
    ones_row = jnp.where(pad_row == 0, 1.0, 0.0).astype(BF16)
    for h in range(N_HEADS):
        k_ref[h] = k[:, h * V_DIM:(h + 1) * V_DIM].astype(BF16)
        qt_ref[h] = (qv_t[h * V_DIM:(h + 1) * V_DIM, :] * scale).astype(BF16)
        vt_ref[h, :V_DIM] = qv_t[ATTN_WIDTH + h * V_DIM:ATTN_WIDTH + (h + 1) * V_DIM, :].astype(BF16)
        vt_ref[h, V_DIM:] = ones_row
    gb_ref[...] = c[:, :CONV_WIDTH]
    u_ref[...] = c[:, CONV_WIDTH:2 * CONV_WIDTH] * c[:, 2 * CONV_WIDTH:]


def _project(x, wk, wqvt, wc, tile):
    b, s, _ = x.shape
    nt = s // tile
    const = lambda shape: pl.BlockSpec(shape, lambda bi, ti: (0,) * len(shape),
                                       pipeline_mode=pl.Buffered(1))
    head_major = lambda shape: pl.BlockSpec((None, N_HEADS, None) + shape,
                                            lambda bi, ti: (bi, 0, ti, 0, 0))
    rows = lambda width: pl.BlockSpec((None, tile, width), lambda bi, ti: (bi, ti, 0))
    return pl.pallas_call(
        _proj_kernel,
        grid=(b, nt),
        in_specs=[rows(D_MODEL), const(wk.shape), const(wqvt.shape), const(wc.shape)],
        out_specs=[head_major((tile, V_DIM)), head_major((V_DIM, tile)),
                   head_major((V_ROWS, tile)), rows(CONV_WIDTH), rows(CONV_WIDTH)],
        out_shape=[jax.ShapeDtypeStruct((b, N_HEADS, nt, tile, V_DIM), BF16),
                   jax.ShapeDtypeStruct((b, N_HEADS, nt, V_DIM, tile), BF16),
                   jax.ShapeDtypeStruct((b, N_HEADS, nt, V_ROWS, tile), BF16),
                   jax.ShapeDtypeStruct((b, s, CONV_WIDTH), F32),
                   jax.ShapeDtypeStruct((b, s, CONV_WIDTH), F32)],
        compiler_params=pltpu.CompilerParams(
            dimension_semantics=("parallel", "parallel"),
            vmem_limit_bytes=VMEM_LIMIT_BYTES),
        name="in_proj",
    )(x, wk, wqvt, wc)


def _attn_kernel(linit_ref, qt_ref, k_ref, vt_ref, bias_ref, lamv_ref, g_ref,
                 o_ref, qz_ref, s_ref, smax_ref, m_ref, acc_ref, *, n_tiles, unroll):
    qi = pl.program_id(2)

    qt = qt_ref[...].astype(F32)
    row = lax.broadcasted_iota(jnp.int32, qt.shape, 0)
    qz_ref[0] = jnp.where(row < HEAD_DIM, qt, 0.0).astype(BF16)
    qz_ref[1] = jnp.where(row >= HEAD_DIM, qt, 0.0).astype(BF16)
    m_ref[...] = jnp.full(m_ref.shape, NEG_BIG, F32)
    acc_ref[...] = jnp.zeros(acc_ref.shape, F32)

    def scores(ki, slot):
        kt = k_ref[ki]
        bias_t = bias_ref[jnp.clip(ki - qi, -BIAS_REACH, BIAS_REACH) + BIAS_REACH]
        for mp in range(2):
            s = jnp.dot(kt, qz_ref[mp], preferred_element_type=F32) + bias_t
            s_ref[slot, mp] = s
            smax_ref[slot, mp] = jnp.max(s, axis=0, keepdims=True)

    def softmax_pv(ki, slot):
        vt = vt_ref[ki]
        for mp in range(2):
            m_prev = m_ref[mp]
            m_new = jnp.maximum(m_prev, smax_ref[slot, mp])
            p = jnp.exp2((s_ref[slot, mp] - m_new).astype(BF16))
            acc_ref[mp] = jnp.exp2(m_prev - m_new) * acc_ref[mp] + jnp.dot(
                vt, p, preferred_element_type=F32)
            m_ref[mp] = m_new

    def unrolled_iterations(jj, carry):
        for u in range(unroll):
            j = jj * unroll + u
            scores(jnp.minimum(j + 1, n_tiles - 1), (u + 1) % 2)
            softmax_pv(j, u % 2)
        return carry

    scores(0, 0)
    lax.fori_loop(0, n_tiles // unroll, unrolled_iterations, 0)

    lamv = lamv_ref[...]
    lam = (jnp.exp(jnp.sum(lamv[0:1] * lamv[1:2], axis=1, keepdims=True))
           - jnp.exp(jnp.sum(lamv[2:3] * lamv[3:4], axis=1, keepdims=True))
           + linit_ref[0])
    o = (acc_ref[0, :V_DIM] / acc_ref[0, V_DIM:V_DIM + 1]
         - lam * (acc_ref[1, :V_DIM] / acc_ref[1, V_DIM:V_DIM + 1]))
    o = o * lax.rsqrt(jnp.mean(jnp.square(o), axis=0, keepdims=True) + LN_EPS)
    o = o * g_ref[...] * (1.0 - linit_ref[0])
    o_ref[...] = o.T.astype(o_ref.dtype)


def _attention(linit, q_t, k, v_t, bias, lamv, g_col, seq_len):
    b, _, n_tiles, _, tile = q_t.shape
    unroll = math.gcd(n_tiles, KEY_TILES_PER_TRIP)
    assert unroll % 2 == 0
    smem = pl.BlockSpec(memory_space=pltpu.SMEM)
    per_head_seq = lambda shape: pl.BlockSpec((None, None, n_tiles) + shape,
                                              lambda bi, hi, qi: (bi, hi, 0, 0, 0))
    return pl.pallas_call(
        functools.partial(_attn_kernel, n_tiles=n_tiles, unroll=unroll),
        grid=(b, N_HEADS, n_tiles),
        in_specs=[smem,
                  pl.BlockSpec((None, None, None, V_DIM, tile),
                               lambda bi, hi, qi: (bi, hi, qi, 0, 0)),
                  per_head_seq((tile, V_DIM)),
                  per_head_seq((V_ROWS, tile)),
                  pl.BlockSpec((None,) + bias.shape[1:], lambda bi, hi, qi: (hi, 0, 0, 0)),
                  pl.BlockSpec(lamv.shape, lambda bi, hi, qi: (0, 0)),
                  pl.BlockSpec(g_col.shape, lambda bi, hi, qi: (0, 0))],
        out_specs=pl.BlockSpec((None, tile, V_DIM), lambda bi, hi, qi: (bi, qi, hi)),
        out_shape=jax.ShapeDtypeStruct((b, seq_len, ATTN_WIDTH), BF16),
        scratch_shapes=[pltpu.VMEM((2, V_DIM, tile), BF16),
                        pltpu.VMEM((2, 2, tile, tile), F32),
                        pltpu.VMEM((2, 2, 1, tile), F32),
                        pltpu.VMEM((2, 1, tile), F32),
                        pltpu.VMEM((2, V_ROWS, tile), F32)],
        compiler_params=pltpu.CompilerParams(
            dimension_semantics=("parallel", "parallel", "arbitrary"),
            vmem_limit_bytes=VMEM_LIMIT_BYTES),
        name="diff_attn",
    )(linit, q_t, k, v_t, bias, lamv, g_col)


def _layer_norm(y, g, b):
    mu = jnp.mean(y, axis=-1, keepdims=True)
    yc = y - mu
    var = jnp.mean(jnp.square(yc), axis=-1, keepdims=True)
    return yc * lax.rsqrt(var + LN_EPS) * g + b


def _mix_mlp_kernel(attn_ref, gb_ref, u_ref, uprev_ref, unext_ref, x_ref, wout_ref, cw_ref,
                    cb_ref, ln1g_ref, ln1b_ref, w1_ref, w2_ref, ln2g_ref, ln2b_ref, o_ref,
                    *, alpha, n_tiles):
    ti = pl.program_id(1)
    u = u_ref[...]
    tile = u.shape[0]
    prev_row = jnp.where(ti > 0, uprev_ref[HALO_ROWS - 1:HALO_ROWS, :], 0.0)
    next_row = jnp.where(ti < n_tiles - 1, unext_ref[0:1, :], 0.0)
    row = lax.broadcasted_iota(jnp.int32, u.shape, 0)
    u_prev = jnp.where(row == 0, prev_row, pltpu.roll(u, 1, axis=0))
    u_next = jnp.where(row == tile - 1, next_row, pltpu.roll(u, tile - 1, axis=0))
    cw = cw_ref[...]
    conv = cw[0:1] * u_prev + cw[1:2] * u + cw[2:3] * u_next + cb_ref[...]
    conv_out = (gb_ref[...] * conv).astype(BF16)
    mix = (jnp.dot(attn_ref[...], wout_ref[:ATTN_WIDTH, :], preferred_element_type=F32)
           + jnp.dot(conv_out, wout_ref[ATTN_WIDTH:, :], preferred_element_type=F32))
    x1 = _layer_norm(alpha * x_ref[...] + mix, ln1g_ref[...], ln1b_ref[...])
    x1b = x1.astype(BF16)
    ffn = jnp.zeros_like(x1)
    for c0 in range(0, D_FF, FF_CHUNK):
        hid = jnp.dot(x1b, w1_ref[:, c0:c0 + FF_CHUNK], preferred_element_type=F32)
        hid = jnp.square(jnp.maximum(hid, 0.0)).astype(BF16)
        ffn = ffn + jnp.dot(hid, w2_ref[c0:c0 + FF_CHUNK, :], preferred_element_type=F32)
    o_ref[...] = _layer_norm(alpha * x1 + ffn, ln2g_ref[...], ln2b_ref[...])


def _mix_mlp(attn, gb, u, x, wout, cw, cb, ln1g, ln1b, w1, w2, ln2g, ln2b, tile, alpha):
    b, s, _ = x.shape
    nt = s // tile
    halo_per_tile = tile // HALO_ROWS
    n_halo = s // HALO_ROWS
    const = lambda arr: pl.BlockSpec(arr.shape, lambda bi, ti: (0,) * arr.ndim,
                                     pipeline_mode=pl.Buffered(1))
    rows = lambda width: pl.BlockSpec((None, tile, width), lambda bi, ti: (bi, ti, 0))
    return pl.pallas_call(
        functools.partial(_mix_mlp_kernel, alpha=alpha, n_tiles=nt),
        grid=(b, nt),
        in_specs=[rows(ATTN_WIDTH), rows(CONV_WIDTH), rows(CONV_WIDTH),
                  pl.BlockSpec((None, HALO_ROWS, CONV_WIDTH),
                               lambda bi, ti: (bi, jnp.maximum(ti * halo_per_tile - 1, 0), 0)),
                  pl.BlockSpec((None, HALO_ROWS, CONV_WIDTH),
                               lambda bi, ti: (bi, jnp.minimum((ti + 1) * halo_per_tile,
                                                               n_halo - 1), 0)),
                  rows(D_MODEL), const(wout), const(cw), const(cb), const(ln1g), const(ln1b),
                  const(w1), const(w2), const(ln2g), const(ln2b)],
        out_specs=rows(D_MODEL),
        out_shape=jax.ShapeDtypeStruct((b, s, D_MODEL), F32),
        compiler_params=pltpu.CompilerParams(
            dimension_semantics=("parallel", "parallel"),
            vmem_limit_bytes=VMEM_LIMIT_BYTES),
        name="mix_mlp",
    )(attn, gb, u, u, u, x, wout, cw, cb, ln1g, ln1b, w1, w2, ln2g, ln2b)


def _encoder_layer(x, lw, bias, tile, alpha):
    s = x.shape[1]
    k, q_t, v_t, gb, u = _project(x, lw["wk"], lw["wqvt"], lw["wc"], tile)
    attn = _attention(lw["linit"], q_t, k, v_t, bias, lw["lamv"], lw["g_col"], s)
    return _mix_mlp(attn, gb, u, x, lw["wout"], lw["cw"], lw["cb"], lw["ln1g"], lw["ln1b"],
                    lw["w1"], lw["w2"], lw["ln2g"], lw["ln2b"], tile, alpha)


def kernel(x_prompt, x_sample, w_in, w_out, conv_w, conv_b, lambda_q1, lambda_k1, lambda_q2,
           lambda_k2, subln_g, rel_bias, ln1_g, ln1_b, w_mlp1, w_mlp2, ln2_g, ln2_b):
    depth = w_in.shape[0]
    alpha = (2.0 * depth) ** 0.25
    tile = SEQ_TILE
    for x in (x_prompt, x_sample):
        assert x.shape[1] % tile == 0 and x.shape[2] == D_MODEL
    bias = _bias_tables(rel_bias, tile)

    qk = 2 * ATTN_WIDTH
    layers = []
    for l in range(depth):
        wl = w_in[l]
        linit = 0.8 - 0.6 * math.exp(-0.3 * l)
        layers.append(dict(
            wk=wl[:, ATTN_WIDTH:qk].astype(BF16),
            wqvt=jnp.concatenate([wl[:, :ATTN_WIDTH], wl[:, qk:qk + ATTN_WIDTH]],
                                 axis=1).T.astype(BF16),
            wc=wl[:, qk + ATTN_WIDTH:].astype(BF16),
            linit=jnp.full((1,), linit, F32),
            lamv=jnp.stack([lambda_q1[l], lambda_k1[l], lambda_q2[l], lambda_k2[l]]).astype(F32),
            g_col=subln_g[l].astype(F32).reshape(V_DIM, 1),
            wout=w_out[l].astype(BF16),
            cw=conv_w[l].astype(F32), cb=conv_b[l].astype(F32).reshape(1, CONV_WIDTH),
            ln1g=ln1_g[l].astype(F32).reshape(1, D_MODEL),
            ln1b=ln1_b[l].astype(F32).reshape(1, D_MODEL),
            w1=w_mlp1[l].astype(BF16), w2=w_mlp2[l].astype(BF16),
            ln2g=ln2_g[l].astype(F32).reshape(1, D_MODEL),
            ln2b=ln2_b[l].astype(F32).reshape(1, D_MODEL)))

    y_prompt, y_sample = x_prompt, x_sample
    for lw in layers:
        y_prompt = _encoder_layer(y_prompt, lw, bias, tile, alpha)
        y_sample = _encoder_layer(y_sample, lw, bias, tile, alpha)
    return (y_prompt, y_sample)
```

```python
import functools
import math

import jax
import jax.numpy as jnp
from jax import lax
from jax.experimental import pallas as pl
from jax.experimental.pallas import tpu as pltpu

D_MODEL = 1024
HEAD_DIM = 64
V_DIM = 2 * HEAD_DIM
BF16_SUBLANES = 16
V_ROWS = V_DIM + BF16_SUBLANES
N_HEADS = 4
ATTN_WIDTH = N_HEADS * V_DIM
CONV_WIDTH = D_MODEL - ATTN_WIDTH
D_FF = 4 * D_MODEL
NUM_BUCKETS = 32
MAX_DISTANCE = 128
LN_EPS = 1e-5
LOG2_E = math.log2(math.e)
HALO_ROWS = 8
SEQ_TILE = 512
BIAS_REACH = 2
KEY_TILES_PER_TRIP = 2
FF_CHUNK = 1024
VMEM_LIMIT_BYTES = 56 * 1024 * 1024
NEG_BIG = -1e30

F32 = jnp.float32
BF16 = jnp.bfloat16


def _t5_bucket(rel):
    half = NUM_BUCKETS // 2
    max_exact = half // 2
    ret = jnp.where(rel > 0, half, 0)
    n = jnp.abs(rel)
    nf = jnp.maximum(n, 1).astype(F32)
    large = max_exact + (jnp.log(nf / max_exact) / math.log(MAX_DISTANCE / max_exact)
                         * (half - max_exact)).astype(jnp.int32)
    large = jnp.minimum(large, half - 1)
    return ret + jnp.where(n < max_exact, n, large)


def _toeplitz_kernel(u_ref, o_ref):
    tile = o_ref.shape[0]
    x = jnp.broadcast_to(u_ref[...], (tile, 2 * tile))
    o_ref[...] = pltpu.roll(x, tile + 1, 1, stride=1, stride_axis=0)[:, :tile]


def _bias_tables(rel_bias, tile):
    assert tile >= MAX_DISTANCE
    n_d = 2 * BIAS_REACH + 1
    span = (BIAS_REACH + 1) * tile
    rel = jnp.arange(-span, span, dtype=jnp.int32)
    tab = rel_bias.astype(F32)[_t5_bucket(rel)].T * LOG2_E
    i = jnp.arange(2 * tile, dtype=jnp.int32)[None, :]
    d = jnp.arange(-BIAS_REACH, BIAS_REACH + 1, dtype=jnp.int32)[:, None]
    u = tab[:, d * tile + tile - 1 - i + span][:, :, None, :]
    return pl.pallas_call(
        _toeplitz_kernel,
        grid=(N_HEADS, n_d),
        in_specs=[pl.BlockSpec((None, None, 1, 2 * tile), lambda h, dd: (h, dd, 0, 0))],
        out_specs=pl.BlockSpec((None, None, tile, tile), lambda h, dd: (h, dd, 0, 0)),
        out_shape=jax.ShapeDtypeStruct((N_HEADS, n_d, tile, tile), F32),
        name="bias_toeplitz",
    )(u)


def _proj_kernel(x_ref, wk_ref, wqvt_ref, wc_ref, k_ref, qt_ref, vt_ref, gb_ref, u_ref):
    xb = x_ref[...].astype(BF16)
    k = jnp.dot(xb, wk_ref[...], preferred_element_type=F32)
    qv_t = lax.dot_general(wqvt_ref[...], xb, (((1,), (1,)), ((), ())),
                           preferred_element_type=F32)
    c = jnp.dot(xb, wc_ref[...], preferred_element_type=F32)
    scale = HEAD_DIM ** -0.5 * LOG2_E
    pad_row = lax.broadcasted_iota(jnp.int32, (V_ROWS - V_DIM, xb.shape[0]), 0)
    ones_row = jnp.where(pad_row == 0, 1.0, 0.0).astype(BF16)
    for h in range(N_HEADS):
        k_ref[h] = k[:, h * V_DIM:(h + 1) * V_DIM].astype(BF16)
        qt_ref[h] = (qv_t[h * V_DIM:(h + 1) * V_DIM, :] * scale).astype(BF16)
        vt_ref[h, :V_DIM] = qv_t[ATTN_WIDTH + h * V_DIM:ATTN_WIDTH + (h + 1) * V_DIM, :].astype(BF16)
        vt_ref[h, V_DIM:] = ones_row
    gb_ref[...] = c[:, :CONV_WIDTH]
    u_ref[...] = c[:, CONV_WIDTH:2 * CONV_WIDTH] * c[:, 2 * CONV_WIDTH:]


def _project(x, wk, wqvt, wc, tile):
    b, s, _ = x.shape
    nt = s // tile
    const = lambda shape: pl.BlockSpec(shape, lambda bi, ti: (0,) * len(shape),
                                       pipeline_mode=pl.Buffered(1))
    head_major = lambda shape: pl.BlockSpec((None, N_HEADS, None) + shape,
                                            lambda bi, ti: (bi, 0, ti, 0, 0))
    rows = lambda width: pl.BlockSpec((None, tile, width), lambda bi, ti: (bi, ti, 0))
    return pl.pallas_call(
        _proj_kernel,
        grid=(b, nt),
        in_specs=[rows(D_MODEL), const(wk.shape), const(wqvt.shape), const(wc.shape)],
        out_specs=[head_major((tile, V_DIM)), head_major((V_DIM, tile)),
                   head_major((V_ROWS, tile)), rows(CONV_WIDTH), rows(CONV_WIDTH)],
        out_shape=[jax.ShapeDtypeStruct((b, N_HEADS, nt, tile, V_DIM), BF16),
                   jax.ShapeDtypeStruct((b, N_HEADS, nt, V_DIM, tile), BF16),
                   jax.ShapeDtypeStruct((b, N_HEADS, nt, V_ROWS, tile), BF16),
                   jax.ShapeDtypeStruct((b, s, CONV_WIDTH), F32),
                   jax.ShapeDtypeStruct((b, s, CONV_WIDTH), F32)],
        compiler_params=pltpu.CompilerParams(
            dimension_semantics=("parallel", "parallel"),
            vmem_limit_bytes=VMEM_LIMIT_BYTES),
        name="in_proj",
    )(x, wk, wqvt, wc)


def _attn_kernel(linit_ref, qt_ref, k_ref, vt_ref, bias_ref, lamv_ref, g_ref,
                 o_ref, qz_ref, s_ref, smax_ref, p_ref, alpha_ref, m_ref, acc_ref,
                 *, n_tiles, unroll):
    qi = pl.program_id(2)

    qt = qt_ref[...].astype(F32)
    row = lax.broadcasted_iota(jnp.int32, qt.shape, 0)
    qz_ref[0] = jnp.where(row < HEAD_DIM, qt, 0.0).astype(BF16)
    qz_ref[1] = jnp.where(row >= HEAD_DIM, qt, 0.0).astype(BF16)
    m_ref[...] = jnp.full(m_ref.shape, NEG_BIG, F32)
    acc_ref[...] = jnp.zeros(acc_ref.shape, F32)

    p_ref[1] = jnp.zeros(p_ref.shape[1:], BF16)
    alpha_ref[1] = jnp.ones(alpha_ref.shape[1:], F32)

    def scores(ki):
        kt = k_ref[ki]
        bias_t = bias_ref[jnp.clip(ki - qi, -BIAS_REACH, BIAS_REACH) + BIAS_REACH]
        for mp in range(2):
            s = jnp.dot(kt, qz_ref[mp], preferred_element_type=F32) + bias_t
            s_ref[mp] = s
            smax_ref[mp] = jnp.max(s, axis=0, keepdims=True)

    def softmax(slot):
        for mp in range(2):
            m_prev = m_ref[mp]
            m_new = jnp.maximum(m_prev, smax_ref[mp])
            p_ref[slot, mp] = jnp.exp2((s_ref[mp] - m_new).astype(BF16))
            alpha_ref[slot, mp] = jnp.exp2(m_prev - m_new)
            m_ref[mp] = m_new

    def pv(ki, slot):
        vt = vt_ref[ki]
        for mp in range(2):
            acc_ref[mp] = alpha_ref[slot, mp] * acc_ref[mp] + jnp.dot(
                vt, p_ref[slot, mp], preferred_element_type=F32)

    def unrolled_iterations(jj, carry):
        for u in range(unroll):
            j = jj * unroll + u
            softmax(u % 2)
            scores(jnp.minimum(j + 1, n_tiles - 1))
            pv(jnp.maximum(j - 1, 0), (u + 1) % 2)
        return carry

    scores(0)
    lax.fori_loop(0, n_tiles // unroll, unrolled_iterations, 0)
    pv(n_tiles - 1, (n_tiles - 1) % 2)

    lamv = lamv_ref[...]
    lam = (jnp.exp(jnp.sum(lamv[0:1] * lamv[1:2], axis=1, keepdims=True))
           - jnp.exp(jnp.sum(lamv[2:3] * lamv[3:4], axis=1, keepdims=True))
           + linit_ref[0])
    o = (acc_ref[0, :V_DIM] / acc_ref[0, V_DIM:V_DIM + 1]
         - lam * (acc_ref[1, :V_DIM] / acc_ref[1, V_DIM:V_DIM + 1]))
    o = o * lax.rsqrt(jnp.mean(jnp.square(o), axis=0, keepdims=True) + LN_EPS)
    o = o * g_ref[...] * (1.0 - linit_ref[0])
    o_ref[...] = o.T.astype(o_ref.dtype)


def _attention(linit, q_t, k, v_t, bias, lamv, g_col, seq_len):
    b, _, n_tiles, _, tile = q_t.shape
    unroll = math.gcd(n_tiles, KEY_TILES_PER_TRIP)
    assert unroll % 2 == 0
    smem = pl.BlockSpec(memory_space=pltpu.SMEM)
    per_head_seq = lambda shape: pl.BlockSpec((None, None, n_tiles) + shape,
                                              lambda bi, hi, qi: (bi, hi, 0, 0, 0))
    return pl.pallas_call(
        functools.partial(_attn_kernel, n_tiles=n_tiles, unroll=unroll),
        grid=(b, N_HEADS, n_tiles),
        in_specs=[smem,
                  pl.BlockSpec((None, None, None, V_DIM, tile),
                               lambda bi, hi, qi: (bi, hi, qi, 0, 0)),
                  per_head_seq((tile, V_DIM)),
                  per_head_seq((V_ROWS, tile)),
                  pl.BlockSpec((None,) + bias.shape[1:], lambda bi, hi, qi: (hi, 0, 0, 0)),
                  pl.BlockSpec(lamv.shape, lambda bi, hi, qi: (0, 0)),
                  pl.BlockSpec(g_col.shape, lambda bi, hi, qi: (0, 0))],
        out_specs=pl.BlockSpec((None, tile, V_DIM), lambda bi, hi, qi: (bi, qi, hi)),
        out_shape=jax.ShapeDtypeStruct((b, seq_len, ATTN_WIDTH), BF16),
        scratch_shapes=[pltpu.VMEM((2, V_DIM, tile), BF16),
                        pltpu.VMEM((2, tile, tile), F32),
                        pltpu.VMEM((2, 1, tile), F32),
                        pltpu.VMEM((2, 2, tile, tile), BF16),
                        pltpu.VMEM((2, 2, 1, tile), F32),
                        pltpu.VMEM((2, 1, tile), F32),
                        pltpu.VMEM((2, V_ROWS, tile), F32)],
        compiler_params=pltpu.CompilerParams(
            dimension_semantics=("parallel", "parallel", "arbitrary"),
            vmem_limit_bytes=VMEM_LIMIT_BYTES),
        name="diff_attn",
    )(linit, q_t, k, v_t, bias, lamv, g_col)


def _layer_norm(y, g, b):
    mu = jnp.mean(y, axis=-1, keepdims=True)
    yc = y - mu
    var = jnp.mean(jnp.square(yc), axis=-1, keepdims=True)
    return yc * lax.rsqrt(var + LN_EPS) * g + b


def _mix_mlp_kernel(attn_ref, gb_ref, u_ref, uprev_ref, unext_ref, x_ref, wout_ref, cw_ref,
                    cb_ref, ln1g_ref, ln1b_ref, w1_ref, w2_ref, ln2g_ref, ln2b_ref, o_ref,
                    *, alpha, n_tiles):
    ti = pl.program_id(1)
    u = u_ref[...]
    tile = u.shape[0]
    prev_row = jnp.where(ti > 0, uprev_ref[HALO_ROWS - 1:HALO_ROWS, :], 0.0)
    next_row = jnp.where(ti < n_tiles - 1, unext_ref[0:1, :], 0.0)
    row = lax.broadcasted_iota(jnp.int32, u.shape, 0)
    u_prev = jnp.where(row == 0, prev_row, pltpu.roll(u, 1, axis=0))
    u_next = jnp.where(row == tile - 1, next_row, pltpu.roll(u, tile - 1, axis=0))
    cw = cw_ref[...]
    conv = cw[0:1] * u_prev + cw[1:2] * u + cw[2:3] * u_next + cb_ref[...]
    conv_out = (gb_ref[...] * conv).astype(BF16)
    mix = (jnp.dot(attn_ref[...], wout_ref[:ATTN_WIDTH, :], preferred_element_type=F32)
           + jnp.dot(conv_out, wout_ref[ATTN_WIDTH:, :], preferred_element_type=F32))
    x1 = _layer_norm(alpha * x_ref[...] + mix, ln1g_ref[...], ln1b_ref[...])
    x1b = x1.astype(BF16)
    ffn = jnp.zeros_like(x1)
    for c0 in range(0, D_FF, FF_CHUNK):
        hid = jnp.dot(x1b, w1_ref[:, c0:c0 + FF_CHUNK], preferred_element_type=F32)
        hid = jnp.square(jnp.maximum(hid, 0.0)).astype(BF16)
        ffn = ffn + jnp.dot(hid, w2_ref[c0:c0 + FF_CHUNK, :], preferred_element_type=F32)
    o_ref[...] = _layer_norm(alpha * x1 + ffn, ln2g_ref[...], ln2b_ref[...])


def _mix_mlp(attn, gb, u, x, wout, cw, cb, ln1g, ln1b, w1, w2, ln2g, ln2b, tile, alpha):
    b, s, _ = x.shape
    nt = s // tile
    halo_per_tile = tile // HALO_ROWS
    n_halo = s // HALO_ROWS
    const = lambda arr: pl.BlockSpec(arr.shape, lambda bi, ti: (0,) * arr.ndim,
                                     pipeline_mode=pl.Buffered(1))
    rows = lambda width: pl.BlockSpec((None, tile, width), lambda bi, ti: (bi, ti, 0))
    return pl.pallas_call(
        functools.partial(_mix_mlp_kernel, alpha=alpha, n_tiles=nt),
        grid=(b, nt),
        in_specs=[rows(ATTN_WIDTH), rows(CONV_WIDTH), rows(CONV_WIDTH),
                  pl.BlockSpec((None, HALO_ROWS, CONV_WIDTH),
                               lambda bi, ti: (bi, jnp.maximum(ti * halo_per_tile - 1, 0), 0)),
                  pl.BlockSpec((None, HALO_ROWS, CONV_WIDTH),
                               lambda bi, ti: (bi, jnp.minimum((ti + 1) * halo_per_tile,
                                                               n_halo - 1), 0)),
                  rows(D_MODEL), const(wout), const(cw), const(cb), const(ln1g), const(ln1b),
                  const(w1), const(w2), const(ln2g), const(ln2b)],
        out_specs=rows(D_MODEL),
        out_shape=jax.ShapeDtypeStruct((b, s, D_MODEL), F32),
        compiler_params=pltpu.CompilerParams(
            dimension_semantics=("parallel", "parallel"),
            vmem_limit_bytes=VMEM_LIMIT_BYTES),
        name="mix_mlp",
    )(attn, gb, u, u, u, x, wout, cw, cb, ln1g, ln1b, w1, w2, ln2g, ln2b)


def _encoder_layer(x, lw, bias, tile, alpha):
    s = x.shape[1]
    k, q_t, v_t, gb, u = _project(x, lw["wk"], lw["wqvt"], lw["wc"], tile)
    attn = _attention(lw["linit"], q_t, k, v_t, bias, lw["lamv"], lw["g_col"], s)
    return _mix_mlp(attn, gb, u, x, lw["wout"], lw["cw"], lw["cb"], lw["ln1g"], lw["ln1b"],
                    lw["w1"], lw["w2"], lw["ln2g"], lw["ln2b"], tile, alpha)


def kernel(x_prompt, x_sample, w_in, w_out, conv_w, conv_b, lambda_q1, lambda_k1, lambda_q2,
           lambda_k2, subln_g, rel_bias, ln1_g, ln1_b, w_mlp1, w_mlp2, ln2_g, ln2_b):
    depth = w_in.shape[0]
    alpha = (2.0 * depth) ** 0.25
    tile = SEQ_TILE
    for x in (x_prompt, x_sample):
        assert x.shape[1] % tile == 0 and x.shape[2] == D_MODEL
    bias = _bias_tables(rel_bias, tile)

    qk = 2 * ATTN_WIDTH
    layers = []
    for l in range(depth):
        wl = w_in[l]
        linit = 0.8 - 0.6 * math.exp(-0.3 * l)
        layers.append(dict(
            wk=wl[:, ATTN_WIDTH:qk].astype(BF16),
            wqvt=jnp.concatenate([wl[:, :ATTN_WIDTH], wl[:, qk:qk + ATTN_WIDTH]],
                                 axis=1).T.astype(BF16),
            wc=wl[:, qk + ATTN_WIDTH:].astype(BF16),
            linit=jnp.full((1,), linit, F32),
            lamv=jnp.stack([lambda_q1[l], lambda_k1[l], lambda_q2[l], lambda_k2[l]]).astype(F32),
            g_col=subln_g[l].astype(F32).reshape(V_DIM, 1),
            wout=w_out[l].astype(BF16),
            cw=conv_w[l].astype(F32), cb=conv_b[l].astype(F32).reshape(1, CONV_WIDTH),
            ln1g=ln1_g[l].astype(F32).reshape(1, D_MODEL),
            ln1b=ln1_b[l].astype(F32).reshape(1, D_MODEL),
            w1=w_mlp1[l].astype(BF16), w2=w_mlp2[l].astype(BF16),
            ln2g=ln2_g[l].astype(F32).reshape(1, D_MODEL),
            ln2b=ln2_b[l].astype(F32).reshape(1, D_MODEL)))

    y_prompt, y_sample = x_prompt, x_sample
    for lw in layers:
        y_prompt = _encoder_layer(y_prompt, lw, bias, tile, alpha)
        y_sample = _encoder_layer(y_sample, lw, bias, tile, alpha)
    return (y_prompt, y_sample)
```

```python
import functools
import math

import jax
import jax.numpy as jnp
from jax import lax
from jax.experimental import pallas as pl
from jax.experimental.pallas import tpu as pltpu

D_MODEL = 1024
HEAD_DIM = 64
V_DIM = 2 * HEAD_DIM
BF16_SUBLANES = 16
V_ROWS = V_DIM + BF16_SUBLANES
N_HEADS = 4
ATTN_WIDTH = N_HEADS * V_DIM
CONV_WIDTH = D_MODEL - ATTN_WIDTH
D_FF = 4 * D_MODEL
NUM_BUCKETS = 32
MAX_DISTANCE = 128
LN_EPS = 1e-5
LOG2_E = math.log2(math.e)
HALO_ROWS = 8
SEQ_TILE = 512
ATTN_TILE_FACTOR = 2
BIAS_BLOCK = 128
BIAS_REACH = 2
KEY_TILES_PER_TRIP = 2
FF_CHUNK = 1024
VMEM_LIMIT_BYTES = 56 * 1024 * 1024
NEG_BIG = -1e30

F32 = jnp.float32
BF16 = jnp.bfloat16


def _t5_bucket(rel):
    half = NUM_BUCKETS // 2
    max_exact = half // 2
    ret = jnp.where(rel > 0, half, 0)
    n = jnp.abs(rel)
    nf = jnp.maximum(n, 1).astype(F32)
    large = max_exact + (jnp.log(nf / max_exact) / math.log(MAX_DISTANCE / max_exact)
                         * (half - max_exact)).astype(jnp.int32)
    large = jnp.minimum(large, half - 1)
    return ret + jnp.where(n < max_exact, n, large)


def _toeplitz_kernel(u_ref, o_ref):
    n = o_ref.shape[0]
    x = jnp.broadcast_to(u_ref[...], (n, 2 * n))
    o_ref[...] = pltpu.roll(x, n + 1, 1, stride=1, stride_axis=0)[:, :n]


def _bias_blocks(rel_bias):
    assert BIAS_BLOCK >= MAX_DISTANCE and BIAS_REACH >= 2
    n = BIAS_BLOCK
    n_d = 2 * BIAS_REACH + 1
    span = (BIAS_REACH + 1) * n
    rel = jnp.arange(-span, span, dtype=jnp.int32)
    tab = rel_bias.astype(F32)[_t5_bucket(rel)].T * LOG2_E
    i = jnp.arange(2 * n, dtype=jnp.int32)[None, :]
    d = jnp.arange(-BIAS_REACH, BIAS_REACH + 1, dtype=jnp.int32)[:, None]
    u = tab[:, d * n + n - 1 - i + span][:, :, None, :]
    return pl.pallas_call(
        _toeplitz_kernel,
        grid=(N_HEADS, n_d),
        in_specs=[pl.BlockSpec((None, None, 1, 2 * n), lambda h, dd: (h, dd, 0, 0))],
        out_specs=pl.BlockSpec((None, None, n, n), lambda h, dd: (h, dd, 0, 0)),
        out_shape=jax.ShapeDtypeStruct((N_HEADS, n_d, n, n), F32),
        name="bias_toeplitz",
    )(u)


def _proj_kernel(x_ref, wk_ref, wqvt_ref, wc_ref, k_ref, qt_ref, vt_ref, gb_ref, u_ref):
    xb = x_ref[...].astype(BF16)
    k = jnp.dot(xb, wk_ref[...], preferred_element_type=F32)
    qv_t = lax.dot_general(wqvt_ref[...], xb, (((1,), (1,)), ((), ())),
                           preferred_element_type=F32)
    c = jnp.dot(xb, wc_ref[...], preferred_element_type=F32)
    scale = HEAD_DIM ** -0.5 * LOG2_E
    pad_row = lax.broadcasted_iota(jnp.int32, (V_ROWS - V_DIM, xb.shape[0]), 0)
    ones_row = jnp.where(pad_row == 0, 1.0, 0.0).astype(BF16)
    for h in range(N_HEADS):
        k_ref[h] = k[:, h * V_DIM:(h + 1) * V_DIM].astype(BF16)
        qt_ref[h] = (qv_t[h * V_DIM:(h + 1) * V_DIM, :] * scale).astype(BF16)
        vt_ref[h, :V_DIM] = qv_t[ATTN_WIDTH + h * V_DIM:ATTN_WIDTH + (h + 1) * V_DIM, :].astype(BF16)
        vt_ref[h, V_DIM:] = ones_row
    gb_ref[...] = c[:, :CONV_WIDTH]
    u_ref[...] = c[:, CONV_WIDTH:2 * CONV_WIDTH] * c[:, 2 * CONV_WIDTH:]


def _project(x, wk, wqvt, wc, tile):
    b, s, _ = x.shape
    nt = s // tile
    const = lambda shape: pl.BlockSpec(shape, lambda bi, ti: (0,) * len(shape),
                                       pipeline_mode=pl.Buffered(1))
    head_major = lambda shape: pl.BlockSpec((None, N_HEADS, None) + shape,
                                            lambda bi, ti: (bi, 0, ti, 0, 0))
    rows = lambda width: pl.BlockSpec((None, tile, width), lambda bi, ti: (bi, ti, 0))
    return pl.pallas_call(
        _proj_kernel,
        grid=(b, nt),
        in_specs=[rows(D_MODEL), const(wk.shape), const(wqvt.shape), const(wc.shape)],
        out_specs=[head_major((tile, V_DIM)), head_major((V_DIM, tile)),
                   head_major((V_ROWS, tile)), rows(CONV_WIDTH), rows(CONV_WIDTH)],
        out_shape=[jax.ShapeDtypeStruct((b, N_HEADS, nt, tile, V_DIM), BF16),
                   jax.ShapeDtypeStruct((b, N_HEADS, nt, V_DIM, tile), BF16),
                   jax.ShapeDtypeStruct((b, N_HEADS, nt, V_ROWS, tile), BF16),
                   jax.ShapeDtypeStruct((b, s, CONV_WIDTH), F32),
                   jax.ShapeDtypeStruct((b, s, CONV_WIDTH), F32)],
        compiler_params=pltpu.CompilerParams(
            dimension_semantics=("parallel", "parallel"),
            vmem_limit_bytes=VMEM_LIMIT_BYTES),
        name="in_proj",
    )(x, wk, wqvt, wc)


def _attn_kernel(linit_ref, qt_ref, k_ref, vt_ref, bias_ref, lamv_ref, g_ref,
                 o_ref, qz_ref, s_ref, smax_ref, p_ref, alpha_ref, m_ref, acc_ref,
                 *, n_tiles, unroll):
    qi = pl.program_id(2)
    n_sub = qt_ref.shape[0]
    ta = qz_ref.shape[2]
    n_blk = ta // BIAS_BLOCK

    qt = jnp.concatenate([qt_ref[a] for a in range(n_sub)], axis=1).astype(F32)
    row = lax.broadcasted_iota(jnp.int32, qt.shape, 0)
    qz_ref[0] = jnp.where(row < HEAD_DIM, qt, 0.0).astype(BF16)
    qz_ref[1] = jnp.where(row >= HEAD_DIM, qt, 0.0).astype(BF16)
    m_ref[...] = jnp.full(m_ref.shape, NEG_BIG, F32)
    acc_ref[...] = jnp.zeros(acc_ref.shape, F32)
    p_ref[1] = jnp.zeros(p_ref.shape[1:], BF16)
    alpha_ref[1] = jnp.ones(alpha_ref.shape[1:], F32)

    def bias_tile(ki):
        base = (ki - qi) * n_blk
        blk = {dl: bias_ref[jnp.clip(base + dl, -BIAS_REACH, BIAS_REACH) + BIAS_REACH]
               for dl in range(-(n_blk - 1), n_blk)}
        return jnp.concatenate(
            [jnp.concatenate([blk[i - j] for j in range(n_blk)], axis=1) for i in range(n_blk)],
            axis=0)

    def scores(ki):
        kt = k_ref[ki]
        bias_t = bias_tile(ki)
        for mp in range(2):
            s = jnp.dot(kt, qz_ref[mp], preferred_element_type=F32) + bias_t
            s_ref[mp] = s
            smax_ref[mp] = jnp.max(s, axis=0, keepdims=True)

    def softmax(slot):
        for mp in range(2):
            m_prev = m_ref[mp]
            m_new = jnp.maximum(m_prev, smax_ref[mp])
            p_ref[slot, mp] = jnp.exp2((s_ref[mp] - m_new).astype(BF16))
            alpha_ref[slot, mp] = jnp.exp2(m_prev - m_new)
            m_ref[mp] = m_new

    def pv(ki, slot):
        vt = jnp.concatenate([vt_ref[ki * n_sub + a] for a in range(n_sub)], axis=1)
        for mp in range(2):
            acc_ref[mp] = alpha_ref[slot, mp] * acc_ref[mp] + jnp.dot(
                vt, p_ref[slot, mp], preferred_element_type=F32)

    def unrolled_iterations(jj, carry):
        for u in range(unroll):
            j = jj * unroll + u
            softmax(u % 2)
            scores(jnp.minimum(j + 1, n_tiles - 1))
            pv(jnp.maximum(j - 1, 0), (u + 1) % 2)
        return carry

    scores(0)
    lax.fori_loop(0, n_tiles // unroll, unrolled_iterations, 0)
    pv(n_tiles - 1, (n_tiles - 1) % 2)

    lamv = lamv_ref[...]
    lam = (jnp.exp(jnp.sum(lamv[0:1] * lamv[1:2], axis=1, keepdims=True))
           - jnp.exp(jnp.sum(lamv[2:3] * lamv[3:4], axis=1, keepdims=True))
           + linit_ref[0])
    o = (acc_ref[0, :V_DIM] / acc_ref[0, V_DIM:V_DIM + 1]
         - lam * (acc_ref[1, :V_DIM] / acc_ref[1, V_DIM:V_DIM + 1]))
    o = o * lax.rsqrt(jnp.mean(jnp.square(o), axis=0, keepdims=True) + LN_EPS)
    o = o * g_ref[...] * (1.0 - linit_ref[0])
    o_ref[...] = o.T.astype(o_ref.dtype)


def _attention(linit, q_t, k, v_t, bias, lamv, g_col, seq_len):
    b, _, n_proj_tiles, _, tile = q_t.shape
    n_sub = ATTN_TILE_FACTOR
    ta = n_sub * tile
    assert seq_len % ta == 0 and ta % BIAS_BLOCK == 0
    n_tiles = seq_len // ta
    unroll = math.gcd(n_tiles, KEY_TILES_PER_TRIP)
    assert unroll % 2 == 0
    k = k.reshape(b, N_HEADS, n_tiles, ta, V_DIM)
    smem = pl.BlockSpec(memory_space=pltpu.SMEM)
    return pl.pallas_call(
        functools.partial(_attn_kernel, n_tiles=n_tiles, unroll=unroll),
        grid=(b, N_HEADS, n_tiles),
        in_specs=[smem,
                  pl.BlockSpec((None, None, n_sub, V_DIM, tile),
                               lambda bi, hi, qi: (bi, hi, qi, 0, 0)),
                  pl.BlockSpec((None, None, n_tiles, ta, V_DIM),
                               lambda bi, hi, qi: (bi, hi, 0, 0, 0)),
                  pl.BlockSpec((None, None, n_proj_tiles, V_ROWS, tile),
                               lambda bi, hi, qi: (bi, hi, 0, 0, 0)),
                  pl.BlockSpec((None,) + bias.shape[1:], lambda bi, hi, qi: (hi, 0, 0, 0)),
                  pl.BlockSpec(lamv.shape, lambda bi, hi, qi: (0, 0)),
                  pl.BlockSpec(g_col.shape, lambda bi, hi, qi: (0, 0))],
        out_specs=pl.BlockSpec((None, ta, V_DIM), lambda bi, hi, qi: (bi, qi, hi)),
        out_shape=jax.ShapeDtypeStruct((b, seq_len, ATTN_WIDTH), BF16),
        scratch_shapes=[pltpu.VMEM((2, V_DIM, ta), BF16),
                        pltpu.VMEM((2, ta, ta), F32),
                        pltpu.VMEM((2, 1, ta), F32),
                        pltpu.VMEM((2, 2, ta, ta), BF16),
                        pltpu.VMEM((2, 2, 1, ta), F32),
                        pltpu.VMEM((2, 1, ta), F32),
                        pltpu.VMEM((2, V_ROWS, ta), F32)],
        compiler_params=pltpu.CompilerParams(
            dimension_semantics=("parallel", "parallel", "arbitrary"),
            vmem_limit_bytes=VMEM_LIMIT_BYTES),
        name="diff_attn",
    )(linit, q_t, k, v_t, bias, lamv, g_col)


def _layer_norm(y, g, b):
    mu = jnp.mean(y, axis=-1, keepdims=True)
    yc = y - mu
    var = jnp.mean(jnp.square(yc), axis=-1, keepdims=True)
    return yc * lax.rsqrt(var + LN_EPS) * g + b


def _mix_mlp_kernel(attn_ref, gb_ref, u_ref, uprev_ref, unext_ref, x_ref, wout_ref, cw_ref,
                    cb_ref, ln1g_ref, ln1b_ref, w1_ref, w2_ref, ln2g_ref, ln2b_ref, o_ref,
                    *, alpha, n_tiles):
    ti = pl.program_id(1)
    u = u_ref[...]
    tile = u.shape[0]
    prev_row = jnp.where(ti > 0, uprev_ref[HALO_ROWS - 1:HALO_ROWS, :], 0.0)
    next_row = jnp.where(ti < n_tiles - 1, unext_ref[0:1, :], 0.0)
    row = lax.broadcasted_iota(jnp.int32, u.shape, 0)
    u_prev = jnp.where(row == 0, prev_row, pltpu.roll(u, 1, axis=0))
    u_next = jnp.where(row == tile - 1, next_row, pltpu.roll(u, tile - 1, axis=0))
    cw = cw_ref[...]
    conv = cw[0:1] * u_prev + cw[1:2] * u + cw[2:3] * u_next + cb_ref[...]
    conv_out = (gb_ref[...] * conv).astype(BF16)
    mix = (jnp.dot(attn_ref[...], wout_ref[:ATTN_WIDTH, :], preferred_element_type=F32)
           + jnp.dot(conv_out, wout_ref[ATTN_WIDTH:, :], preferred_element_type=F32))
    x1 = _layer_norm(alpha * x_ref[...] + mix, ln1g_ref[...], ln1b_ref[...])
    x1b = x1.astype(BF16)
    ffn = jnp.zeros_like(x1)
    for c0 in range(0, D_FF, FF_CHUNK):
        hid = jnp.dot(x1b, w1_ref[:, c0:c0 + FF_CHUNK], preferred_element_type=F32)
        hid = jnp.square(jnp.maximum(hid, 0.0)).astype(BF16)
        ffn = ffn + jnp.dot(hid, w2_ref[c0:c0 + FF_CHUNK, :], preferred_element_type=F32)
    o_ref[...] = _layer_norm(alpha * x1 + ffn, ln2g_ref[...], ln2b_ref[...])


def _mix_mlp(attn, gb, u, x, wout, cw, cb, ln1g, ln1b, w1, w2, ln2g, ln2b, tile, alpha):
    b, s, _ = x.shape
    nt = s // tile
    halo_per_tile = tile // HALO_ROWS
    n_halo = s // HALO_ROWS
    const = lambda arr: pl.BlockSpec(arr.shape, lambda bi, ti: (0,) * arr.ndim,
                                     pipeline_mode=pl.Buffered(1))
    rows = lambda width: pl.BlockSpec((None, tile, width), lambda bi, ti: (bi, ti, 0))
    return pl.pallas_call(
        functools.partial(_mix_mlp_kernel, alpha=alpha, n_tiles=nt),
        grid=(b, nt),
        in_specs=[rows(ATTN_WIDTH), rows(CONV_WIDTH), rows(CONV_WIDTH),
                  pl.BlockSpec((None, HALO_ROWS, CONV_WIDTH),
                               lambda bi, ti: (bi, jnp.maximum(ti * halo_per_tile - 1, 0), 0)),
                  pl.BlockSpec((None, HALO_ROWS, CONV_WIDTH),
                               lambda bi, ti: (bi, jnp.minimum((ti + 1) * halo_per_tile,
                                                               n_halo - 1), 0)),
                  rows(D_MODEL), const(wout), const(cw), const(cb), const(ln1g), const(ln1b),
                  const(w1), const(w2), const(ln2g), const(ln2b)],
        out_specs=rows(D_MODEL),
        out_shape=jax.ShapeDtypeStruct((b, s, D_MODEL), F32),
        compiler_params=pltpu.CompilerParams(
            dimension_semantics=("parallel", "parallel"),
            vmem_limit_bytes=VMEM_LIMIT_BYTES),
        name="mix_mlp",
    )(attn, gb, u, u, u, x, wout, cw, cb, ln1g, ln1b, w1, w2, ln2g, ln2b)


def _encoder_layer(x, lw, bias, tile, alpha):
    s = x.shape[1]
    k, q_t, v_t, gb, u = _project(x, lw["wk"], lw["wqvt"], lw["wc"], tile)
    attn = _attention(lw["linit"], q_t, k, v_t, bias, lw["lamv"], lw["g_col"], s)
    return _mix_mlp(attn, gb, u, x, lw["wout"], lw["cw"], lw["cb"], lw["ln1g"], lw["ln1b"],
                    lw["w1"], lw["w2"], lw["ln2g"], lw["ln2b"], tile, alpha)


def kernel(x_prompt, x_sample, w_in, w_out, conv_w, conv_b, lambda_q1, lambda_k1, lambda_q2,
           lambda_k2, subln_g, rel_bias, ln1_g, ln1_b, w_mlp1, w_mlp2, ln2_g, ln2_b):
    depth = w_in.shape[0]
    alpha = (2.0 * depth) ** 0.25
    tile = SEQ_TILE
    for x in (x_prompt, x_sample):
        assert x.shape[1] % (ATTN_TILE_FACTOR * tile) == 0 and x.shape[2] == D_MODEL
    bias = _bias_blocks(rel_bias)

    qk = 2 * ATTN_WIDTH
    layers = []
    for l in range(depth):
        wl = w_in[l]
        linit = 0.8 - 0.6 * math.exp(-0.3 * l)
        layers.append(dict(
            wk=wl[:, ATTN_WIDTH:qk].astype(BF16),
            wqvt=jnp.concatenate([wl[:, :ATTN_WIDTH], wl[:, qk:qk + ATTN_WIDTH]],
                                 axis=1).T.astype(BF16),
            wc=wl[:, qk + ATTN_WIDTH:].astype(BF16),
            linit=jnp.full((1,), linit, F32),
            lamv=jnp.stack([lambda_q1[l], lambda_k1[l], lambda_q2[l], lambda_k2[l]]).astype(F32),
            g_col=subln_g[l].astype(F32).reshape(V_DIM, 1),
            wout=w_out[l].astype(BF16),
            cw=conv_w[l].astype(F32), cb=conv_b[l].astype(F32).reshape(1, CONV_WIDTH),
            ln1g=ln1_g[l].astype(F32).reshape(1, D_MODEL),
            ln1b=ln1_b[l].astype(F32).reshape(1, D_MODEL),
            w1=w_mlp1[l].astype(BF16), w2=w_mlp2[l].astype(BF16),
            ln2g=ln2_g[l].astype(F32).reshape(1, D_MODEL),
            ln2b=ln2_b[l].astype(F32).reshape(1, D_MODEL)))

    y_prompt, y_sample = x_prompt, x_sample
    for lw in layers:
        y_prompt = _encoder_layer(y_prompt, lw, bias, tile, alpha)
        y_sample = _encoder_layer(y_sample, lw, bias, tile, alpha)
    return (y_prompt, y_sample)
```

```python
import functools
import math

import jax
import jax.numpy as jnp
from jax import lax
from jax.experimental import pallas as pl
from jax.experimental.pallas import tpu as pltpu

D_MODEL = 1024
HEAD_DIM = 64
V_DIM = 2 * HEAD_DIM
BF16_SUBLANES = 16
V_ROWS = V_DIM + BF16_SUBLANES
N_HEADS = 4
ATTN_WIDTH = N_HEADS * V_DIM
CONV_WIDTH = D_MODEL - ATTN_WIDTH
D_FF = 4 * D_MODEL
NUM_BUCKETS = 32
MAX_DISTANCE = 128
LN_EPS = 1e-5
LOG2_E = math.log2(math.e)
HALO_ROWS = 8
SEQ_TILE = 512
ATTN_TILE_FACTOR = 2
BIAS_BLOCK = 128
BIAS_REACH = 2
FF_CHUNK = 1024
VMEM_LIMIT_BYTES = 56 * 1024 * 1024
NEG_BIG = -1e30

F32 = jnp.float32
BF16 = jnp.bfloat16


def _t5_bucket(rel):
    half = NUM_BUCKETS // 2
    max_exact = half // 2
    ret = jnp.where(rel > 0, half, 0)
    n = jnp.abs(rel)
    nf = jnp.maximum(n, 1).astype(F32)
    large = max_exact + (jnp.log(nf / max_exact) / math.log(MAX_DISTANCE / max_exact)
                         * (half - max_exact)).astype(jnp.int32)
    large = jnp.minimum(large, half - 1)
    return ret + jnp.where(n < max_exact, n, large)


def _toeplitz_kernel(u_ref, o_ref):
    n = o_ref.shape[0]
    x = jnp.broadcast_to(u_ref[...], (n, 2 * n))
    o_ref[...] = pltpu.roll(x, n + 1, 1, stride=1, stride_axis=0)[:, :n]


def _bias_blocks(rel_bias):
    assert BIAS_BLOCK >= MAX_DISTANCE and BIAS_REACH >= 2
    n = BIAS_BLOCK
    n_d = 2 * BIAS_REACH + 1
    span = (BIAS_REACH + 1) * n
    rel = jnp.arange(-span, span, dtype=jnp.int32)
    tab = rel_bias.astype(F32)[_t5_bucket(rel)].T * LOG2_E
    i = jnp.arange(2 * n, dtype=jnp.int32)[None, :]
    d = jnp.arange(-BIAS_REACH, BIAS_REACH + 1, dtype=jnp.int32)[:, None]
    u = tab[:, d * n + n - 1 - i + span][:, :, None, :]
    return pl.pallas_call(
        _toeplitz_kernel,
        grid=(N_HEADS, n_d),
        in_specs=[pl.BlockSpec((None, None, 1, 2 * n), lambda h, dd: (h, dd, 0, 0))],
        out_specs=pl.BlockSpec((None, None, n, n), lambda h, dd: (h, dd, 0, 0)),
        out_shape=jax.ShapeDtypeStruct((N_HEADS, n_d, n, n), F32),
        name="bias_toeplitz",
    )(u)


def _proj_kernel(x_ref, wk_ref, wqvt_ref, wc_ref, k_ref, qt_ref, vt_ref, gb_ref, u_ref):
    xb = x_ref[...].astype(BF16)
    k = jnp.dot(xb, wk_ref[...], preferred_element_type=F32)
    qv_t = lax.dot_general(wqvt_ref[...], xb, (((1,), (1,)), ((), ())),
                           preferred_element_type=F32)
    c = jnp.dot(xb, wc_ref[...], preferred_element_type=F32)
    scale = HEAD_DIM ** -0.5 * LOG2_E
    pad_row = lax.broadcasted_iota(jnp.int32, (V_ROWS - V_DIM, xb.shape[0]), 0)
    ones_row = jnp.where(pad_row == 0, 1.0, 0.0).astype(BF16)
    for h in range(N_HEADS):
        k_ref[h] = k[:, h * V_DIM:(h + 1) * V_DIM].astype(BF16)
        qt_ref[h] = (qv_t[h * V_DIM:(h + 1) * V_DIM, :] * scale).astype(BF16)
        vt_ref[h, :V_DIM] = qv_t[ATTN_WIDTH + h * V_DIM:ATTN_WIDTH + (h + 1) * V_DIM, :].astype(BF16)
        vt_ref[h, V_DIM:] = ones_row
    gb_ref[...] = c[:, :CONV_WIDTH]
    u_ref[...] = c[:, CONV_WIDTH:2 * CONV_WIDTH] * c[:, 2 * CONV_WIDTH:]


def _project(x, wk, wqvt, wc, tile):
    b, s, _ = x.shape
    nt = s // tile
    const = lambda shape: pl.BlockSpec(shape, lambda bi, ti: (0,) * len(shape),
                                       pipeline_mode=pl.Buffered(1))
    head_major = lambda shape: pl.BlockSpec((None, N_HEADS, None) + shape,
                                            lambda bi, ti: (bi, 0, ti, 0, 0))
    rows = lambda width: pl.BlockSpec((None, tile, width), lambda bi, ti: (bi, ti, 0))
    return pl.pallas_call(
        _proj_kernel,
        grid=(b, nt),
        in_specs=[rows(D_MODEL), const(wk.shape), const(wqvt.shape), const(wc.shape)],
        out_specs=[head_major((tile, V_DIM)), head_major((V_DIM, tile)),
                   head_major((V_ROWS, tile)), rows(CONV_WIDTH), rows(CONV_WIDTH)],
        out_shape=[jax.ShapeDtypeStruct((b, N_HEADS, nt, tile, V_DIM), BF16),
                   jax.ShapeDtypeStruct((b, N_HEADS, nt, V_DIM, tile), BF16),
                   jax.ShapeDtypeStruct((b, N_HEADS, nt, V_ROWS, tile), BF16),
                   jax.ShapeDtypeStruct((b, s, CONV_WIDTH), F32),
                   jax.ShapeDtypeStruct((b, s, CONV_WIDTH), F32)],
        compiler_params=pltpu.CompilerParams(
            dimension_semantics=("parallel", "parallel"),
            vmem_limit_bytes=VMEM_LIMIT_BYTES),
        name="in_proj",
    )(x, wk, wqvt, wc)


def _attn_kernel(linit_ref, qt_ref, k_ref, vt_ref, bias_ref, lamv_ref, g_ref,
                 o_ref, qz_ref, s_ref, smax_ref, p_ref, alpha_ref, m_ref, acc_ref,
                 *, n_tiles):
    qi = pl.program_id(2)
    n_sub = qt_ref.shape[0]
    ta = qz_ref.shape[2]
    n_blk = ta // BIAS_BLOCK

    qt = jnp.concatenate([qt_ref[a] for a in range(n_sub)], axis=1).astype(F32)
    row = lax.broadcasted_iota(jnp.int32, qt.shape, 0)
    qz_ref[0] = jnp.where(row < HEAD_DIM, qt, 0.0).astype(BF16)
    qz_ref[1] = jnp.where(row >= HEAD_DIM, qt, 0.0).astype(BF16)
    m_ref[...] = jnp.full(m_ref.shape, NEG_BIG, F32)
    acc_ref[...] = jnp.zeros(acc_ref.shape, F32)

    def bias_tile(ki):
        base = (ki - qi) * n_blk
        blk = {dl: bias_ref[jnp.clip(base + dl, -BIAS_REACH, BIAS_REACH) + BIAS_REACH]
               for dl in range(-(n_blk - 1), n_blk)}
        return jnp.concatenate(
            [jnp.concatenate([blk[i - j] for j in range(n_blk)], axis=1) for i in range(n_blk)],
            axis=0)

    def scores(ki):
        kt = k_ref[ki]
        bias_t = bias_tile(ki)
        for mp in range(2):
            s = jnp.dot(kt, qz_ref[mp], preferred_element_type=F32) + bias_t
            s_ref[mp] = s
            smax_ref[mp] = jnp.max(s, axis=0, keepdims=True)

    def softmax(slot):
        for mp in range(2):
            m_prev = m_ref[mp]
            m_new = jnp.maximum(m_prev, smax_ref[mp])
            p_ref[slot, mp] = jnp.exp2((s_ref[mp] - m_new).astype(BF16))
            alpha_ref[slot, mp] = jnp.exp2(m_prev - m_new)
            m_ref[mp] = m_new

    def pv(ki, slot):
        vt = jnp.concatenate([vt_ref[ki * n_sub + a] for a in range(n_sub)], axis=1)
        for mp in range(2):
            acc_ref[mp] = alpha_ref[slot, mp] * acc_ref[mp] + jnp.dot(
                vt, p_ref[slot, mp], preferred_element_type=F32)

    def two_iterations(jj, carry):
        for u in range(2):
            j = 1 + 2 * jj + u
            softmax((1 + u) % 2)
            scores(j + 1)
            pv(j - 1, u % 2)
        return carry

    scores(0)
    softmax(0)
    scores(1)
    lax.fori_loop(0, (n_tiles - 2) // 2, two_iterations, 0)
    softmax(1)
    pv(n_tiles - 2, 0)
    pv(n_tiles - 1, 1)

    lamv = lamv_ref[...]
    lam = (jnp.exp(jnp.sum(lamv[0:1] * lamv[1:2], axis=1, keepdims=True))
           - jnp.exp(jnp.sum(lamv[2:3] * lamv[3:4], axis=1, keepdims=True))
           + linit_ref[0])
    o = (acc_ref[0, :V_DIM] / acc_ref[0, V_DIM:V_DIM + 1]
         - lam * (acc_ref[1, :V_DIM] / acc_ref[1, V_DIM:V_DIM + 1]))
    o = o * lax.rsqrt(jnp.mean(jnp.square(o), axis=0, keepdims=True) + LN_EPS)
    o = o * g_ref[...] * (1.0 - linit_ref[0])
    o_ref[...] = o.T.astype(o_ref.dtype)


def _attention(linit, q_t, k, v_t, bias, lamv, g_col, seq_len):
    b, _, n_proj_tiles, _, tile = q_t.shape
    n_sub = ATTN_TILE_FACTOR
    ta = n_sub * tile
    assert seq_len % ta == 0 and ta % BIAS_BLOCK == 0
    n_tiles = seq_len // ta
    assert n_tiles % 2 == 0
    k = k.reshape(b, N_HEADS, n_tiles, ta, V_DIM)
    smem = pl.BlockSpec(memory_space=pltpu.SMEM)
    return pl.pallas_call(
        functools.partial(_attn_kernel, n_tiles=n_tiles),
        grid=(b, N_HEADS, n_tiles),
        in_specs=[smem,
                  pl.BlockSpec((None, None, n_sub, V_DIM, tile),
                               lambda bi, hi, qi: (bi, hi, qi, 0, 0)),
                  pl.BlockSpec((None, None, n_tiles, ta, V_DIM),
                               lambda bi, hi, qi: (bi, hi, 0, 0, 0)),
                  pl.BlockSpec((None, None, n_proj_tiles, V_ROWS, tile),
                               lambda bi, hi, qi: (bi, hi, 0, 0, 0)),
                  pl.BlockSpec((None,) + bias.shape[1:], lambda bi, hi, qi: (hi, 0, 0, 0)),
                  pl.BlockSpec(lamv.shape, lambda bi, hi, qi: (0, 0)),
                  pl.BlockSpec(g_col.shape, lambda bi, hi, qi: (0, 0))],
        out_specs=pl.BlockSpec((None, ta, V_DIM), lambda bi, hi, qi: (bi, qi, hi)),
        out_shape=jax.ShapeDtypeStruct((b, seq_len, ATTN_WIDTH), BF16),
        scratch_shapes=[pltpu.VMEM((2, V_DIM, ta), BF16),
                        pltpu.VMEM((2, ta, ta), F32),
                        pltpu.VMEM((2, 1, ta), F32),
                        pltpu.VMEM((2, 2, ta, ta), BF16),
                        pltpu.VMEM((2, 2, 1, ta), F32),
                        pltpu.VMEM((2, 1, ta), F32),
                        pltpu.VMEM((2, V_ROWS, ta), F32)],
        compiler_params=pltpu.CompilerParams(
            dimension_semantics=("parallel", "parallel", "arbitrary"),
            vmem_limit_bytes=VMEM_LIMIT_BYTES),
        name="diff_attn",
    )(linit, q_t, k, v_t, bias, lamv, g_col)


def _layer_norm(y, g, b):
    mu = jnp.mean(y, axis=-1, keepdims=True)
    yc = y - mu
    var = jnp.mean(jnp.square(yc), axis=-1, keepdims=True)
    return yc * lax.rsqrt(var + LN_EPS) * g + b


def _mix_mlp_kernel(attn_ref, gb_ref, u_ref, uprev_ref, unext_ref, x_ref, wout_ref, cw_ref,
                    cb_ref, ln1g_ref, ln1b_ref, w1_ref, w2_ref, ln2g_ref, ln2b_ref, o_ref,
                    *, alpha, n_tiles):
    ti = pl.program_id(1)
    u = u_ref[...]
    tile = u.shape[0]
    prev_row = jnp.where(ti > 0, uprev_ref[HALO_ROWS - 1:HALO_ROWS, :], 0.0)
    next_row = jnp.where(ti < n_tiles - 1, unext_ref[0:1, :], 0.0)
    row = lax.broadcasted_iota(jnp.int32, u.shape, 0)
    u_prev = jnp.where(row == 0, prev_row, pltpu.roll(u, 1, axis=0))
    u_next = jnp.where(row == tile - 1, next_row, pltpu.roll(u, tile - 1, axis=0))
    cw = cw_ref[...]
    conv = cw[0:1] * u_prev + cw[1:2] * u + cw[2:3] * u_next + cb_ref[...]
    conv_out = (gb_ref[...] * conv).astype(BF16)
    mix = (jnp.dot(attn_ref[...], wout_ref[:ATTN_WIDTH, :], preferred_element_type=F32)
           + jnp.dot(conv_out, wout_ref[ATTN_WIDTH:, :], preferred_element_type=F32))
    x1 = _layer_norm(alpha * x_ref[...] + mix, ln1g_ref[...], ln1b_ref[...])
    x1b = x1.astype(BF16)
    ffn = jnp.zeros_like(x1)
    for c0 in range(0, D_FF, FF_CHUNK):
        hid = jnp.dot(x1b, w1_ref[:, c0:c0 + FF_CHUNK], preferred_element_type=F32)
        hid = jnp.square(jnp.maximum(hid, 0.0)).astype(BF16)
        ffn = ffn + jnp.dot(hid, w2_ref[c0:c0 + FF_CHUNK, :], preferred_element_type=F32)
    o_ref[...] = _layer_norm(alpha * x1 + ffn, ln2g_ref[...], ln2b_ref[...])


def _mix_mlp(attn, gb, u, x, wout, cw, cb, ln1g, ln1b, w1, w2, ln2g, ln2b, tile, alpha):
    b, s, _ = x.shape
    nt = s // tile
    halo_per_tile = tile // HALO_ROWS
    n_halo = s // HALO_ROWS
    const = lambda arr: pl.BlockSpec(arr.shape, lambda bi, ti: (0,) * arr.ndim,
                                     pipeline_mode=pl.Buffered(1))
    rows = lambda width: pl.BlockSpec((None, tile, width), lambda bi, ti: (bi, ti, 0))
    return pl.pallas_call(
        functools.partial(_mix_mlp_kernel, alpha=alpha, n_tiles=nt),
        grid=(b, nt),
        in_specs=[rows(ATTN_WIDTH), rows(CONV_WIDTH), rows(CONV_WIDTH),
                  pl.BlockSpec((None, HALO_ROWS, CONV_WIDTH),
                               lambda bi, ti: (bi, jnp.maximum(ti * halo_per_tile - 1, 0), 0)),
                  pl.BlockSpec((None, HALO_ROWS, CONV_WIDTH),
                               lambda bi, ti: (bi, jnp.minimum((ti + 1) * halo_per_tile,
                                                               n_halo - 1), 0)),
                  rows(D_MODEL), const(wout), const(cw), const(cb), const(ln1g), const(ln1b),
                  const(w1), const(w2), const(ln2g), const(ln2b)],
        out_specs=rows(D_MODEL),
        out_shape=jax.ShapeDtypeStruct((b, s, D_MODEL), F32),
        compiler_params=pltpu.CompilerParams(
            dimension_semantics=("parallel", "parallel"),
            vmem_limit_bytes=VMEM_LIMIT_BYTES),
        name="mix_mlp",
    )(attn, gb, u, u, u, x, wout, cw, cb, ln1g, ln1b, w1, w2, ln2g, ln2b)


def _encoder_layer(x, lw, bias, tile, alpha):
    s = x.shape[1]
    k, q_t, v_t, gb, u = _project(x, lw["wk"], lw["wqvt"], lw["wc"], tile)
    attn = _attention(lw["linit"], q_t, k, v_t, bias, lw["lamv"], lw["g_col"], s)
    return _mix_mlp(attn, gb, u, x, lw["wout"], lw["cw"], lw["cb"], lw["ln1g"], lw["ln1b"],
                    lw["w1"], lw["w2"], lw["ln2g"], lw["ln2b"], tile, alpha)


def kernel(x_prompt, x_sample, w_in, w_out, conv_w, conv_b, lambda_q1, lambda_k1, lambda_q2,
           lambda_k2, subln_g, rel_bias, ln1_g, ln1_b, w_mlp1, w_mlp2, ln2_g, ln2_b):
    depth = w_in.shape[0]
    alpha = (2.0 * depth) ** 0.25
    tile = SEQ_TILE
    for x in (x_prompt, x_sample):
        assert x.shape[1] % (ATTN_TILE_FACTOR * tile) == 0 and x.shape[2] == D_MODEL
    bias = _bias_blocks(rel_bias)

    qk = 2 * ATTN_WIDTH
    layers = []
    for l in range(depth):
        wl = w_in[l]
        linit = 0.8 - 0.6 * math.exp(-0.3 * l)
        layers.append(dict(
            wk=wl[:, ATTN_WIDTH:qk].astype(BF16),
            wqvt=jnp.concatenate([wl[:, :ATTN_WIDTH], wl[:, qk:qk + ATTN_WIDTH]],
                                 axis=1).T.astype(BF16),
            wc=wl[:, qk + ATTN_WIDTH:].astype(BF16),
            linit=jnp.full((1,), linit, F32),
            lamv=jnp.stack([lambda_q1[l], lambda_k1[l], lambda_q2[l], lambda_k2[l]]).astype(F32),
            g_col=subln_g[l].astype(F32).reshape(V_DIM, 1),
            wout=w_out[l].astype(BF16),
            cw=conv_w[l].astype(F32), cb=conv_b[l].astype(F32).reshape(1, CONV_WIDTH),
            ln1g=ln1_g[l].astype(F32).reshape(1, D_MODEL),
            ln1b=ln1_b[l].astype(F32).reshape(1, D_MODEL),
            w1=w_mlp1[l].astype(BF16), w2=w_mlp2[l].astype(BF16),
            ln2g=ln2_g[l].astype(F32).reshape(1, D_MODEL),
            ln2b=ln2_b[l].astype(F32).reshape(1, D_MODEL)))

    y_prompt, y_sample = x_prompt, x_sample
    for lw in layers:
        y_prompt = _encoder_layer(y_prompt, lw, bias, tile, alpha)
        y_sample = _encoder_layer(y_sample, lw, bias, tile, alpha)
    return (y_prompt, y_sample)
```

```python
import functools
import math

import jax
import jax.numpy as jnp
from jax import lax
from jax.experimental import pallas as pl
from jax.experimental.pallas import tpu as pltpu

D_MODEL = 1024
HEAD_DIM = 64
V_DIM = 2 * HEAD_DIM
BF16_SUBLANES = 16
V_ROWS = V_DIM + BF16_SUBLANES
N_HEADS = 4
ATTN_WIDTH = N_HEADS * V_DIM
CONV_WIDTH = D_MODEL - ATTN_WIDTH
D_FF = 4 * D_MODEL
NUM_BUCKETS = 32
MAX_DISTANCE = 128
LN_EPS = 1e-5
LOG2_E = math.log2(math.e)
HALO_ROWS = 8
SEQ_TILE = 512
ATTN_TILE_FACTOR = 2
BIAS_BLOCK = 128
BIAS_REACH = 2
FF_CHUNK = 1024
VMEM_LIMIT_BYTES = 56 * 1024 * 1024
NEG_BIG = -1e30

F32 = jnp.float32
BF16 = jnp.bfloat16


def _t5_bucket(rel):
    half = NUM_BUCKETS // 2
    max_exact = half // 2
    ret = jnp.where(rel > 0, half, 0)
    n = jnp.abs(rel)
    nf = jnp.maximum(n, 1).astype(F32)
    large = max_exact + (jnp.log(nf / max_exact) / math.log(MAX_DISTANCE / max_exact)
                         * (half - max_exact)).astype(jnp.int32)
    large = jnp.minimum(large, half - 1)
    return ret + jnp.where(n < max_exact, n, large)


def _toeplitz_kernel(u_ref, o_ref):
    n = o_ref.shape[0]
    x = jnp.broadcast_to(u_ref[...], (n, 2 * n))
    o_ref[...] = pltpu.roll(x, n + 1, 1, stride=1, stride_axis=0)[:, :n]


def _bias_blocks(rel_bias):
    assert BIAS_BLOCK >= MAX_DISTANCE and BIAS_REACH >= 2
    n = BIAS_BLOCK
    n_d = 2 * BIAS_REACH + 1
    span = (BIAS_REACH + 1) * n
    rel = jnp.arange(-span, span, dtype=jnp.int32)
    tab = rel_bias.astype(F32)[_t5_bucket(rel)].T * LOG2_E
    i = jnp.arange(2 * n, dtype=jnp.int32)[None, :]
    d = jnp.arange(-BIAS_REACH, BIAS_REACH + 1, dtype=jnp.int32)[:, None]
    u = tab[:, d * n + n - 1 - i + span][:, :, None, :]
    return pl.pallas_call(
        _toeplitz_kernel,
        grid=(N_HEADS, n_d),
        in_specs=[pl.BlockSpec((None, None, 1, 2 * n), lambda h, dd: (h, dd, 0, 0))],
        out_specs=pl.BlockSpec((None, None, n, n), lambda h, dd: (h, dd, 0, 0)),
        out_shape=jax.ShapeDtypeStruct((N_HEADS, n_d, n, n), F32),
        name="bias_toeplitz",
    )(u)


def _proj_kernel(x_ref, wk_ref, wqvt_ref, wc_ref, k_ref, qt_ref, vt_ref, gb_ref, u_ref):
    xb = x_ref[...].astype(BF16)
    k = jnp.dot(xb, wk_ref[...], preferred_element_type=F32)
    qv_t = lax.dot_general(wqvt_ref[...], xb, (((1,), (1,)), ((), ())),
                           preferred_element_type=F32)
    c = jnp.dot(xb, wc_ref[...], preferred_element_type=F32)
    scale = HEAD_DIM ** -0.5 * LOG2_E
    pad_row = lax.broadcasted_iota(jnp.int32, (V_ROWS - V_DIM, xb.shape[0]), 0)
    ones_row = jnp.where(pad_row == 0, 1.0, 0.0).astype(BF16)
    for h in range(N_HEADS):
        k_ref[h] = k[:, h * V_DIM:(h + 1) * V_DIM].astype(BF16)
        qt_ref[h] = (qv_t[h * V_DIM:(h + 1) * V_DIM, :] * scale).astype(BF16)
        vt_ref[h, :V_DIM] = qv_t[ATTN_WIDTH + h * V_DIM:ATTN_WIDTH + (h + 1) * V_DIM, :].astype(BF16)
        vt_ref[h, V_DIM:] = ones_row
    gb_ref[...] = c[:, :CONV_WIDTH]
    u_ref[...] = c[:, CONV_WIDTH:2 * CONV_WIDTH] * c[:, 2 * CONV_WIDTH:]


def _project(x, wk, wqvt, wc, tile):
    b, s, _ = x.shape
    nt = s // tile
    const = lambda shape: pl.BlockSpec(shape, lambda bi, ti: (0,) * len(shape),
                                       pipeline_mode=pl.Buffered(1))
    head_major = lambda shape: pl.BlockSpec((None, N_HEADS, None) + shape,
                                            lambda bi, ti: (bi, 0, ti, 0, 0))
    rows = lambda width: pl.BlockSpec((None, tile, width), lambda bi, ti: (bi, ti, 0))
    return pl.pallas_call(
        _proj_kernel,
        grid=(b, nt),
        in_specs=[rows(D_MODEL), const(wk.shape), const(wqvt.shape), const(wc.shape)],
        out_specs=[head_major((tile, V_DIM)), head_major((V_DIM, tile)),
                   head_major((V_ROWS, tile)), rows(CONV_WIDTH), rows(CONV_WIDTH)],
        out_shape=[jax.ShapeDtypeStruct((b, N_HEADS, nt, tile, V_DIM), BF16),
                   jax.ShapeDtypeStruct((b, N_HEADS, nt, V_DIM, tile), BF16),
                   jax.ShapeDtypeStruct((b, N_HEADS, nt, V_ROWS, tile), BF16),
                   jax.ShapeDtypeStruct((b, s, CONV_WIDTH), F32),
                   jax.ShapeDtypeStruct((b, s, CONV_WIDTH), F32)],
        compiler_params=pltpu.CompilerParams(
            dimension_semantics=("parallel", "parallel"),
            vmem_limit_bytes=VMEM_LIMIT_BYTES),
        name="in_proj",
    )(x, wk, wqvt, wc)


def _attn_kernel(linit_ref, trips_ref, qt_ref, qtn_ref, k_ref, vt_ref, bias_ref, lamv_ref, g_ref,
                 o_ref, qz_ref, s_ref, smax_ref, p_ref, alpha_ref, m_ref, acc_ref,
                 *, n_tiles):
    qi = pl.program_id(2)
    n_sub = qt_ref.shape[0]
    ta = qz_ref.shape[3]
    n_blk = ta // BIAS_BLOCK

    for which, src_ref in enumerate((qt_ref, qtn_ref)):
        qt = jnp.concatenate([src_ref[a] for a in range(n_sub)], axis=1).astype(F32)
        row = lax.broadcasted_iota(jnp.int32, qt.shape, 0)
        qz_ref[which, 0] = jnp.where(row < HEAD_DIM, qt, 0.0).astype(BF16)
        qz_ref[which, 1] = jnp.where(row >= HEAD_DIM, qt, 0.0).astype(BF16)
    m_ref[...] = jnp.full(m_ref.shape, NEG_BIG, F32)
    acc_ref[...] = jnp.zeros(acc_ref.shape, F32)

    def bias_tile(ki, q_idx):
        base = (ki - q_idx) * n_blk
        blk = {dl: bias_ref[jnp.clip(base + dl, -BIAS_REACH, BIAS_REACH) + BIAS_REACH]
               for dl in range(-(n_blk - 1), n_blk)}
        return jnp.concatenate(
            [jnp.concatenate([blk[i - j] for j in range(n_blk)], axis=1) for i in range(n_blk)],
            axis=0)

    def scores(ki, which=0):
        kt = k_ref[ki]
        bias_t = bias_tile(ki, qi + which)
        for mp in range(2):
            s = jnp.dot(kt, qz_ref[which, mp], preferred_element_type=F32) + bias_t
            s_ref[mp] = s
            smax_ref[mp] = jnp.max(s, axis=0, keepdims=True)

    def softmax(slot):
        for mp in range(2):
            m_prev = m_ref[mp]
            m_new = jnp.maximum(m_prev, smax_ref[mp])
            p_ref[slot, mp] = jnp.exp2((s_ref[mp] - m_new).astype(BF16))
            alpha_ref[slot, mp] = jnp.exp2(m_prev - m_new)
            m_ref[mp] = m_new

    def pv(ki, slot):
        vt = jnp.concatenate([vt_ref[ki * n_sub + a] for a in range(n_sub)], axis=1)
        for mp in range(2):
            acc_ref[mp] = alpha_ref[slot, mp] * acc_ref[mp] + jnp.dot(
                vt, p_ref[slot, mp], preferred_element_type=F32)

    def two_iterations(jj, carry):
        for u in range(2):
            j = 1 + 2 * jj + u
            softmax((1 + u) % 2)
            scores(j + 1)
            pv(j - 1, u % 2)
        return carry

    pl.when(qi == 0)(functools.partial(scores, 0))
    softmax(0)
    scores(1)
    lax.fori_loop(0, trips_ref[0], two_iterations, 0)
    softmax(1)
    scores(0, which=1)
    pv(n_tiles - 2, 0)
    pv(n_tiles - 1, 1)

    lamv = lamv_ref[...]
    lam = (jnp.exp(jnp.sum(lamv[0:1] * lamv[1:2], axis=1, keepdims=True))
           - jnp.exp(jnp.sum(lamv[2:3] * lamv[3:4], axis=1, keepdims=True))
           + linit_ref[0])
    o = (acc_ref[0, :V_DIM] / acc_ref[0, V_DIM:V_DIM + 1]
         - lam * (acc_ref[1, :V_DIM] / acc_ref[1, V_DIM:V_DIM + 1]))
    o = o * lax.rsqrt(jnp.mean(jnp.square(o), axis=0, keepdims=True) + LN_EPS)
    o = o * g_ref[...] * (1.0 - linit_ref[0])
    o_ref[...] = o.T.astype(o_ref.dtype)


def _attention(linit, q_t, k, v_t, bias, lamv, g_col, seq_len):
    b, _, n_proj_tiles, _, tile = q_t.shape
    n_sub = ATTN_TILE_FACTOR
    ta = n_sub * tile
    assert seq_len % ta == 0 and ta % BIAS_BLOCK == 0
    n_tiles = seq_len // ta
    assert n_tiles % 2 == 0
    k = k.reshape(b, N_HEADS, n_tiles, ta, V_DIM)
    smem = pl.BlockSpec(memory_space=pltpu.SMEM)
    return pl.pallas_call(
        functools.partial(_attn_kernel, n_tiles=n_tiles),
        grid=(b, N_HEADS, n_tiles),
        in_specs=[smem, smem,
                  pl.BlockSpec((None, None, n_sub, V_DIM, tile),
                               lambda bi, hi, qi: (bi, hi, qi, 0, 0)),
                  pl.BlockSpec((None, None, n_sub, V_DIM, tile),
                               lambda bi, hi, qi: (bi, hi, jnp.minimum(qi + 1, n_tiles - 1), 0, 0)),
                  pl.BlockSpec((None, None, n_tiles, ta, V_DIM),
                               lambda bi, hi, qi: (bi, hi, 0, 0, 0)),
                  pl.BlockSpec((None, None, n_proj_tiles, V_ROWS, tile),
                               lambda bi, hi, qi: (bi, hi, 0, 0, 0)),
                  pl.BlockSpec((None,) + bias.shape[1:], lambda bi, hi, qi: (hi, 0, 0, 0)),
                  pl.BlockSpec(lamv.shape, lambda bi, hi, qi: (0, 0)),
                  pl.BlockSpec(g_col.shape, lambda bi, hi, qi: (0, 0))],
        out_specs=pl.BlockSpec((None, ta, V_DIM), lambda bi, hi, qi: (bi, qi, hi)),
        out_shape=jax.ShapeDtypeStruct((b, seq_len, ATTN_WIDTH), BF16),
        scratch_shapes=[pltpu.VMEM((2, 2, V_DIM, ta), BF16),
                        pltpu.VMEM((2, ta, ta), F32),
                        pltpu.VMEM((2, 1, ta), F32),
                        pltpu.VMEM((2, 2, ta, ta), BF16),
                        pltpu.VMEM((2, 2, 1, ta), F32),
                        pltpu.VMEM((2, 1, ta), F32),
                        pltpu.VMEM((2, V_ROWS, ta), F32)],
        compiler_params=pltpu.CompilerParams(
            dimension_semantics=("parallel", "parallel", "arbitrary"),
            vmem_limit_bytes=VMEM_LIMIT_BYTES),
        name="diff_attn",
    )(linit, jnp.full((1,), (n_tiles - 2) // 2, jnp.int32), q_t, q_t, k, v_t, bias, lamv, g_col)


def _layer_norm(y, g, b):
    mu = jnp.mean(y, axis=-1, keepdims=True)
    yc = y - mu
    var = jnp.mean(jnp.square(yc), axis=-1, keepdims=True)
    return yc * lax.rsqrt(var + LN_EPS) * g + b


def _mix_mlp_kernel(attn_ref, gb_ref, u_ref, uprev_ref, unext_ref, x_ref, wout_ref, cw_ref,
                    cb_ref, ln1g_ref, ln1b_ref, w1_ref, w2_ref, ln2g_ref, ln2b_ref, o_ref,
                    *, alpha, n_tiles):
    ti = pl.program_id(1)
    u = u_ref[...]
    tile = u.shape[0]
    prev_row = jnp.where(ti > 0, uprev_ref[HALO_ROWS - 1:HALO_ROWS, :], 0.0)
    next_row = jnp.where(ti < n_tiles - 1, unext_ref[0:1, :], 0.0)
    row = lax.broadcasted_iota(jnp.int32, u.shape, 0)
    u_prev = jnp.where(row == 0, prev_row, pltpu.roll(u, 1, axis=0))
    u_next = jnp.where(row == tile - 1, next_row, pltpu.roll(u, tile - 1, axis=0))
    cw = cw_ref[...]
    conv = cw[0:1] * u_prev + cw[1:2] * u + cw[2:3] * u_next + cb_ref[...]
    conv_out = (gb_ref[...] * conv).astype(BF16)
    mix = (jnp.dot(attn_ref[...], wout_ref[:ATTN_WIDTH, :], preferred_element_type=F32)
           + jnp.dot(conv_out, wout_ref[ATTN_WIDTH:, :], preferred_element_type=F32))
    x1 = _layer_norm(alpha * x_ref[...] + mix, ln1g_ref[...], ln1b_ref[...])
    x1b = x1.astype(BF16)
    ffn = jnp.zeros_like(x1)
    for c0 in range(0, D_FF, FF_CHUNK):
        hid = jnp.dot(x1b, w1_ref[:, c0:c0 + FF_CHUNK], preferred_element_type=F32)
        hid = jnp.square(jnp.maximum(hid, 0.0)).astype(BF16)
        ffn = ffn + jnp.dot(hid, w2_ref[c0:c0 + FF_CHUNK, :], preferred_element_type=F32)
    o_ref[...] = _layer_norm(alpha * x1 + ffn, ln2g_ref[...], ln2b_ref[...])


def _mix_mlp(attn, gb, u, x, wout, cw, cb, ln1g, ln1b, w1, w2, ln2g, ln2b, tile, alpha):
    b, s, _ = x.shape
    nt = s // tile
    halo_per_tile = tile // HALO_ROWS
    n_halo = s // HALO_ROWS
    const = lambda arr: pl.BlockSpec(arr.shape, lambda bi, ti: (0,) * arr.ndim,
                                     pipeline_mode=pl.Buffered(1))
    rows = lambda width: pl.BlockSpec((None, tile, width), lambda bi, ti: (bi, ti, 0))
    return pl.pallas_call(
        functools.partial(_mix_mlp_kernel, alpha=alpha, n_tiles=nt),
        grid=(b, nt),
        in_specs=[rows(ATTN_WIDTH), rows(CONV_WIDTH), rows(CONV_WIDTH),
                  pl.BlockSpec((None, HALO_ROWS, CONV_WIDTH),
                               lambda bi, ti: (bi, jnp.maximum(ti * halo_per_tile - 1, 0), 0)),
                  pl.BlockSpec((None, HALO_ROWS, CONV_WIDTH),
                               lambda bi, ti: (bi, jnp.minimum((ti + 1) * halo_per_tile,
                                                               n_halo - 1), 0)),
                  rows(D_MODEL), const(wout), const(cw), const(cb), const(ln1g), const(ln1b),
                  const(w1), const(w2), const(ln2g), const(ln2b)],
        out_specs=rows(D_MODEL),
        out_shape=jax.ShapeDtypeStruct((b, s, D_MODEL), F32),
        compiler_params=pltpu.CompilerParams(
            dimension_semantics=("parallel", "parallel"),
            vmem_limit_bytes=VMEM_LIMIT_BYTES),
        name="mix_mlp",
    )(attn, gb, u, u, u, x, wout, cw, cb, ln1g, ln1b, w1, w2, ln2g, ln2b)


def _encoder_layer(x, lw, bias, tile, alpha):
    s = x.shape[1]
    k, q_t, v_t, gb, u = _project(x, lw["wk"], lw["wqvt"], lw["wc"], tile)
    attn = _attention(lw["linit"], q_t, k, v_t, bias, lw["lamv"], lw["g_col"], s)
    return _mix_mlp(attn, gb, u, x, lw["wout"], lw["cw"], lw["cb"], lw["ln1g"], lw["ln1b"],
                    lw["w1"], lw["w2"], lw["ln2g"], lw["ln2b"], tile, alpha)


def kernel(x_prompt, x_sample, w_in, w_out, conv_w, conv_b, lambda_q1, lambda_k1, lambda_q2,
           lambda_k2, subln_g, rel_bias, ln1_g, ln1_b, w_mlp1, w_mlp2, ln2_g, ln2_b):
    depth = w_in.shape[0]
    alpha = (2.0 * depth) ** 0.25
    tile = SEQ_TILE
    for x in (x_prompt, x_sample):
        assert x.shape[1] % (ATTN_TILE_FACTOR * tile) == 0 and x.shape[2] == D_MODEL
    bias = _bias_blocks(rel_bias)

    qk = 2 * ATTN_WIDTH
    layers = []
    for l in range(depth):
        wl = w_in[l]
        linit = 0.8 - 0.6 * math.exp(-0.3 * l)
        layers.append(dict(
            wk=wl[:, ATTN_WIDTH:qk].astype(BF16),
            wqvt=jnp.concatenate([wl[:, :ATTN_WIDTH], wl[:, qk:qk + ATTN_WIDTH]],
                                 axis=1).T.astype(BF16),
            wc=wl[:, qk + ATTN_WIDTH:].astype(BF16),
            linit=jnp.full((1,), linit, F32),
            lamv=jnp.stack([lambda_q1[l], lambda_k1[l], lambda_q2[l], lambda_k2[l]]).astype(F32),
            g_col=subln_g[l].astype(F32).reshape(V_DIM, 1),
            wout=w_out[l].astype(BF16),
            cw=conv_w[l].astype(F32), cb=conv_b[l].astype(F32).reshape(1, CONV_WIDTH),
            ln1g=ln1_g[l].astype(F32).reshape(1, D_MODEL),
            ln1b=ln1_b[l].astype(F32).reshape(1, D_MODEL),
            w1=w_mlp1[l].astype(BF16), w2=w_mlp2[l].astype(BF16),
            ln2g=ln2_g[l].astype(F32).reshape(1, D_MODEL),
            ln2b=ln2_b[l].astype(F32).reshape(1, D_MODEL)))

    y_prompt, y_sample = x_prompt, x_sample
    for lw in layers:
        y_prompt = _encoder_layer(y_prompt, lw, bias, tile, alpha)
        y_sample = _encoder_layer(y_sample, lw, bias, tile, alpha)
    return (y_prompt, y_sample)
```

```python
import functools
import math

import jax
import jax.numpy as jnp
from jax import lax
from jax.experimental import pallas as pl
from jax.experimental.pallas import tpu as pltpu

D_MODEL = 1024
HEAD_DIM = 64
V_DIM = 2 * HEAD_DIM
BF16_SUBLANES = 16
V_ROWS = V_DIM + BF16_SUBLANES
N_HEADS = 4
ATTN_WIDTH = N_HEADS * V_DIM
CONV_WIDTH = D_MODEL - ATTN_WIDTH
D_FF = 4 * D_MODEL
NUM_BUCKETS = 32
MAX_DISTANCE = 128
LN_EPS = 1e-5
LOG2_E = math.log2(math.e)
HALO_ROWS = 8
SEQ_TILE = 512
ATTN_TILE_FACTOR = 2
BIAS_BLOCK = 128
BIAS_REACH = 2
FF_CHUNK = 1024
VMEM_LIMIT_BYTES = 56 * 1024 * 1024
NEG_BIG = -1e30

F32 = jnp.float32
BF16 = jnp.bfloat16


def _t5_bucket(rel):
    half = NUM_BUCKETS // 2
    max_exact = half // 2
    ret = jnp.where(rel > 0, half, 0)
    n = jnp.abs(rel)
    nf = jnp.maximum(n, 1).astype(F32)
    large = max_exact + (jnp.log(nf / max_exact) / math.log(MAX_DISTANCE / max_exact)
                         * (half - max_exact)).astype(jnp.int32)
    large = jnp.minimum(large, half - 1)
    return ret + jnp.where(n < max_exact, n, large)


def _toeplitz_kernel(u_ref, o_ref):
    n = o_ref.shape[0]
    x = jnp.broadcast_to(u_ref[...], (n, 2 * n))
    o_ref[...] = pltpu.roll(x, n + 1, 1, stride=1, stride_axis=0)[:, :n]


def _bias_blocks(rel_bias):
    assert BIAS_BLOCK >= MAX_DISTANCE and BIAS_REACH >= 2
    n = BIAS_BLOCK
    n_d = 2 * BIAS_REACH + 1
    span = (BIAS_REACH + 1) * n
    rel = jnp.arange(-span, span, dtype=jnp.int32)
    tab = rel_bias.astype(F32)[_t5_bucket(rel)].T * LOG2_E
    i = jnp.arange(2 * n, dtype=jnp.int32)[None, :]
    d = jnp.arange(-BIAS_REACH, BIAS_REACH + 1, dtype=jnp.int32)[:, None]
    u = tab[:, d * n + n - 1 - i + span][:, :, None, :]
    return pl.pallas_call(
        _toeplitz_kernel,
        grid=(N_HEADS, n_d),
        in_specs=[pl.BlockSpec((None, None, 1, 2 * n), lambda h, dd: (h, dd, 0, 0))],
        out_specs=pl.BlockSpec((None, None, n, n), lambda h, dd: (h, dd, 0, 0)),
        out_shape=jax.ShapeDtypeStruct((N_HEADS, n_d, n, n), F32),
        name="bias_toeplitz",
    )(u)


def _proj_kernel(x_ref, wk_ref, wqvt_ref, wc_ref, k_ref, qt_ref, vt_ref, gb_ref, u_ref):
    xb = x_ref[...].astype(BF16)
    k = jnp.dot(xb, wk_ref[...], preferred_element_type=F32)
    qv_t = lax.dot_general(wqvt_ref[...], xb, (((1,), (1,)), ((), ())),
                           preferred_element_type=F32)
    c = jnp.dot(xb, wc_ref[...], preferred_element_type=F32)
    scale = HEAD_DIM ** -0.5 * LOG2_E
    pad_row = lax.broadcasted_iota(jnp.int32, (V_ROWS - V_DIM, xb.shape[0]), 0)
    ones_row = jnp.where(pad_row == 0, 1.0, 0.0).astype(BF16)
    for h in range(N_HEADS):
        k_ref[h] = k[:, h * V_DIM:(h + 1) * V_DIM].astype(BF16)
        qt_ref[h] = (qv_t[h * V_DIM:(h + 1) * V_DIM, :] * scale).astype(BF16)
        vt_ref[h, :V_DIM] = qv_t[ATTN_WIDTH + h * V_DIM:ATTN_WIDTH + (h + 1) * V_DIM, :].astype(BF16)
        vt_ref[h, V_DIM:] = ones_row
    gb_ref[...] = c[:, :CONV_WIDTH]
    u_ref[...] = c[:, CONV_WIDTH:2 * CONV_WIDTH] * c[:, 2 * CONV_WIDTH:]


def _project(x, wk, wqvt, wc, tile):
    b, s, _ = x.shape
    nt = s // tile
    const = lambda shape: pl.BlockSpec(shape, lambda bi, ti: (0,) * len(shape),
                                       pipeline_mode=pl.Buffered(1))
    head_major = lambda shape: pl.BlockSpec((None, N_HEADS, None) + shape,
                                            lambda bi, ti: (bi, 0, ti, 0, 0))
    rows = lambda width: pl.BlockSpec((None, tile, width), lambda bi, ti: (bi, ti, 0))
    return pl.pallas_call(
        _proj_kernel,
        grid=(b, nt),
        in_specs=[rows(D_MODEL), const(wk.shape), const(wqvt.shape), const(wc.shape)],
        out_specs=[head_major((tile, V_DIM)), head_major((V_DIM, tile)),
                   head_major((V_ROWS, tile)), rows(CONV_WIDTH), rows(CONV_WIDTH)],
        out_shape=[jax.ShapeDtypeStruct((b, N_HEADS, nt, tile, V_DIM), BF16),
                   jax.ShapeDtypeStruct((b, N_HEADS, nt, V_DIM, tile), BF16),
                   jax.ShapeDtypeStruct((b, N_HEADS, nt, V_ROWS, tile), BF16),
                   jax.ShapeDtypeStruct((b, s, CONV_WIDTH), F32),
                   jax.ShapeDtypeStruct((b, s, CONV_WIDTH), F32)],
        compiler_params=pltpu.CompilerParams(
            dimension_semantics=("parallel", "parallel"),
            vmem_limit_bytes=VMEM_LIMIT_BYTES),
        name="in_proj",
    )(x, wk, wqvt, wc)


def _attn_kernel(linit_ref, trips_ref, qt_ref, qtn_ref, k_ref, vt_ref, bias_ref, lamv_ref, g_ref,
                 o_ref, qz_ref, s_ref, smax_ref, p_ref, alpha_ref, m_ref, acc_ref,
                 *, n_tiles):
    qi = pl.program_id(2)
    n_sub = qt_ref.shape[0]
    ta = qz_ref.shape[3]
    n_blk = ta // BIAS_BLOCK

    for which, src_ref in enumerate((qt_ref, qtn_ref)):
        qt = jnp.concatenate([src_ref[a] for a in range(n_sub)], axis=1).astype(F32)
        row = lax.broadcasted_iota(jnp.int32, qt.shape, 0)
        qz_ref[which, 0] = jnp.where(row < HEAD_DIM, qt, 0.0).astype(BF16)
        qz_ref[which, 1] = jnp.where(row >= HEAD_DIM, qt, 0.0).astype(BF16)
    m_ref[...] = jnp.full(m_ref.shape, NEG_BIG, F32)
    acc_ref[...] = jnp.zeros(acc_ref.shape, F32)

    def bias_tile(ki, q_idx):
        base = (ki - q_idx) * n_blk
        blk = {dl: bias_ref[jnp.clip(base + dl, -BIAS_REACH, BIAS_REACH) + BIAS_REACH]
               for dl in range(-(n_blk - 1), n_blk)}
        return jnp.concatenate(
            [jnp.concatenate([blk[i - j] for j in range(n_blk)], axis=1) for i in range(n_blk)],
            axis=0)

    def scores(ki, which=0):
        kt = k_ref[ki]
        bias_t = bias_tile(ki, qi + which)
        for mp in range(2):
            s = jnp.dot(kt, qz_ref[which, mp], preferred_element_type=F32) + bias_t
            s_ref[mp] = s
            smax_ref[mp] = jnp.max(s, axis=0, keepdims=True)

    def softmax(slot):
        for mp in range(2):
            m_prev = m_ref[mp]
            m_new = jnp.maximum(m_prev, smax_ref[mp])
            p_ref[slot, mp] = jnp.exp2(s_ref[mp] - m_new).astype(BF16)
            alpha_ref[slot, mp] = jnp.exp2(m_prev - m_new)
            m_ref[mp] = m_new

    def pv(ki, slot):
        vt = jnp.concatenate([vt_ref[ki * n_sub + a] for a in range(n_sub)], axis=1)
        for mp in range(2):
            acc_ref[mp] = alpha_ref[slot, mp] * acc_ref[mp] + jnp.dot(
                vt, p_ref[slot, mp], preferred_element_type=F32)

    def two_iterations(jj, carry):
        for u in range(2):
            j = 1 + 2 * jj + u
            softmax((1 + u) % 2)
            scores(j + 1)
            pv(j - 1, u % 2)
        return carry

    pl.when(qi == 0)(functools.partial(scores, 0))
    softmax(0)
    scores(1)
    lax.fori_loop(0, trips_ref[0], two_iterations, 0)
    softmax(1)
    scores(0, which=1)
    pv(n_tiles - 2, 0)
    pv(n_tiles - 1, 1)

    lamv = lamv_ref[...]
    lam = (jnp.exp(jnp.sum(lamv[0:1] * lamv[1:2], axis=1, keepdims=True))
           - jnp.exp(jnp.sum(lamv[2:3] * lamv[3:4], axis=1, keepdims=True))
           + linit_ref[0])
    o = (acc_ref[0, :V_DIM] / acc_ref[0, V_DIM:V_DIM + 1]
         - lam * (acc_ref[1, :V_DIM] / acc_ref[1, V_DIM:V_DIM + 1]))
    o = o * lax.rsqrt(jnp.mean(jnp.square(o), axis=0, keepdims=True) + LN_EPS)
    o = o * g_ref[...] * (1.0 - linit_ref[0])
    o_ref[...] = o.T.astype(o_ref.dtype)


def _attention(linit, q_t, k, v_t, bias, lamv, g_col, seq_len):
    b, _, n_proj_tiles, _, tile = q_t.shape
    n_sub = ATTN_TILE_FACTOR
    ta = n_sub * tile
    assert seq_len % ta == 0 and ta % BIAS_BLOCK == 0
    n_tiles = seq_len // ta
    assert n_tiles % 2 == 0
    k = k.reshape(b, N_HEADS, n_tiles, ta, V_DIM)
    smem = pl.BlockSpec(memory_space=pltpu.SMEM)
    return pl.pallas_call(
        functools.partial(_attn_kernel, n_tiles=n_tiles),
        grid=(b, N_HEADS, n_tiles),
        in_specs=[smem, smem,
                  pl.BlockSpec((None, None, n_sub, V_DIM, tile),
                               lambda bi, hi, qi: (bi, hi, qi, 0, 0)),
                  pl.BlockSpec((None, None, n_sub, V_DIM, tile),
                               lambda bi, hi, qi: (bi, hi, jnp.minimum(qi + 1, n_tiles - 1), 0, 0)),
                  pl.BlockSpec((None, None, n_tiles, ta, V_DIM),
                               lambda bi, hi, qi: (bi, hi, 0, 0, 0)),
                  pl.BlockSpec((None, None, n_proj_tiles, V_ROWS, tile),
                               lambda bi, hi, qi: (bi, hi, 0, 0, 0)),
                  pl.BlockSpec((None,) + bias.shape[1:], lambda bi, hi, qi: (hi, 0, 0, 0)),
                  pl.BlockSpec(lamv.shape, lambda bi, hi, qi: (0, 0)),
                  pl.BlockSpec(g_col.shape, lambda bi, hi, qi: (0, 0))],
        out_specs=pl.BlockSpec((None, ta, V_DIM), lambda bi, hi, qi: (bi, qi, hi)),
        out_shape=jax.ShapeDtypeStruct((b, seq_len, ATTN_WIDTH), BF16),
        scratch_shapes=[pltpu.VMEM((2, 2, V_DIM, ta), BF16),
                        pltpu.VMEM((2, ta, ta), F32),
                        pltpu.VMEM((2, 1, ta), F32),
                        pltpu.VMEM((2, 2, ta, ta), BF16),
                        pltpu.VMEM((2, 2, 1, ta), F32),
                        pltpu.VMEM((2, 1, ta), F32),
                        pltpu.VMEM((2, V_ROWS, ta), F32)],
        compiler_params=pltpu.CompilerParams(
            dimension_semantics=("parallel", "parallel", "arbitrary"),
            vmem_limit_bytes=VMEM_LIMIT_BYTES),
        name="diff_attn",
    )(linit, jnp.full((1,), (n_tiles - 2) // 2, jnp.int32), q_t, q_t, k, v_t, bias, lamv, g_col)


def _layer_norm(y, g, b):
    mu = jnp.mean(y, axis=-1, keepdims=True)
    yc = y - mu
    var = jnp.mean(jnp.square(yc), axis=-1, keepdims=True)
    return yc * lax.rsqrt(var + LN_EPS) * g + b


def _mix_mlp_kernel(attn_ref, gb_ref, u_ref, uprev_ref, unext_ref, x_ref, wout_ref, cw_ref,
                    cb_ref, ln1g_ref, ln1b_ref, w1_ref, w2_ref, ln2g_ref, ln2b_ref, o_ref,
                    *, alpha, n_tiles):
    ti = pl.program_id(1)
    u = u_ref[...]
    tile = u.shape[0]
    prev_row = jnp.where(ti > 0, uprev_ref[HALO_ROWS - 1:HALO_ROWS, :], 0.0)
    next_row = jnp.where(ti < n_tiles - 1, unext_ref[0:1, :], 0.0)
    row = lax.broadcasted_iota(jnp.int32, u.shape, 0)
    u_prev = jnp.where(row == 0, prev_row, pltpu.roll(u, 1, axis=0))
    u_next = jnp.where(row == tile - 1, next_row, pltpu.roll(u, tile - 1, axis=0))
    cw = cw_ref[...]
    conv = cw[0:1] * u_prev + cw[1:2] * u + cw[2:3] * u_next + cb_ref[...]
    conv_out = (gb_ref[...] * conv).astype(BF16)
    mix = (jnp.dot(attn_ref[...], wout_ref[:ATTN_WIDTH, :], preferred_element_type=F32)
           + jnp.dot(conv_out, wout_ref[ATTN_WIDTH:, :], preferred_element_type=F32))
    x1 = _layer_norm(alpha * x_ref[...] + mix, ln1g_ref[...], ln1b_ref[...])
    x1b = x1.astype(BF16)
    ffn = jnp.zeros_like(x1)
    for c0 in range(0, D_FF, FF_CHUNK):
        hid = jnp.dot(x1b, w1_ref[:, c0:c0 + FF_CHUNK], preferred_element_type=F32)
        hid = jnp.square(jnp.maximum(hid, 0.0)).astype(BF16)
        ffn = ffn + jnp.dot(hid, w2_ref[c0:c0 + FF_CHUNK, :], preferred_element_type=F32)
    o_ref[...] = _layer_norm(alpha * x1 + ffn, ln2g_ref[...], ln2b_ref[...])


def _mix_mlp(attn, gb, u, x, wout, cw, cb, ln1g, ln1b, w1, w2, ln2g, ln2b, tile, alpha):
    b, s, _ = x.shape
    nt = s // tile
    halo_per_tile = tile // HALO_ROWS
    n_halo = s // HALO_ROWS
    const = lambda arr: pl.BlockSpec(arr.shape, lambda bi, ti: (0,) * arr.ndim,
                                     pipeline_mode=pl.Buffered(1))
    rows = lambda width: pl.BlockSpec((None, tile, width), lambda bi, ti: (bi, ti, 0))
    return pl.pallas_call(
        functools.partial(_mix_mlp_kernel, alpha=alpha, n_tiles=nt),
        grid=(b, nt),
        in_specs=[rows(ATTN_WIDTH), rows(CONV_WIDTH), rows(CONV_WIDTH),
                  pl.BlockSpec((None, HALO_ROWS, CONV_WIDTH),
                               lambda bi, ti: (bi, jnp.maximum(ti * halo_per_tile - 1, 0), 0)),
                  pl.BlockSpec((None, HALO_ROWS, CONV_WIDTH),
                               lambda bi, ti: (bi, jnp.minimum((ti + 1) * halo_per_tile,
                                                               n_halo - 1), 0)),
                  rows(D_MODEL), const(wout), const(cw), const(cb), const(ln1g), const(ln1b),
                  const(w1), const(w2), const(ln2g), const(ln2b)],
        out_specs=rows(D_MODEL),
        out_shape=jax.ShapeDtypeStruct((b, s, D_MODEL), F32),
        compiler_params=pltpu.CompilerParams(
            dimension_semantics=("parallel", "parallel"),
            vmem_limit_bytes=VMEM_LIMIT_BYTES),
        name="mix_mlp",
    )(attn, gb, u, u, u, x, wout, cw, cb, ln1g, ln1b, w1, w2, ln2g, ln2b)


def _encoder_layer(x, lw, bias, tile, alpha):
    s = x.shape[1]
    k, q_t, v_t, gb, u = _project(x, lw["wk"], lw["wqvt"], lw["wc"], tile)
    attn = _attention(lw["linit"], q_t, k, v_t, bias, lw["lamv"], lw["g_col"], s)
    return _mix_mlp(attn, gb, u, x, lw["wout"], lw["cw"], lw["cb"], lw["ln1g"], lw["ln1b"],
                    lw["w1"], lw["w2"], lw["ln2g"], lw["ln2b"], tile, alpha)


def kernel(x_prompt, x_sample, w_in, w_out, conv_w, conv_b, lambda_q1, lambda_k1, lambda_q2,
           lambda_k2, subln_g, rel_bias, ln1_g, ln1_b, w_mlp1, w_mlp2, ln2_g, ln2_b):
    depth = w_in.shape[0]
    alpha = (2.0 * depth) ** 0.25
    tile = SEQ_TILE
    for x in (x_prompt, x_sample):
        assert x.shape[1] % (ATTN_TILE_FACTOR * tile) == 0 and x.shape[2] == D_MODEL
    bias = _bias_blocks(rel_bias)

    qk = 2 * ATTN_WIDTH
    layers = []
    for l in range(depth):
        wl = w_in[l]
        linit = 0.8 - 0.6 * math.exp(-0.3 * l)
        layers.append(dict(
            wk=wl[:, ATTN_WIDTH:qk].astype(BF16),
            wqvt=jnp.concatenate([wl[:, :ATTN_WIDTH], wl[:, qk:qk + ATTN_WIDTH]],
                                 axis=1).T.astype(BF16),
            wc=wl[:, qk + ATTN_WIDTH:].astype(BF16),
            linit=jnp.full((1,), linit, F32),
            lamv=jnp.stack([lambda_q1[l], lambda_k1[l], lambda_q2[l], lambda_k2[l]]).astype(F32),
            g_col=subln_g[l].astype(F32).reshape(V_DIM, 1),
            wout=w_out[l].astype(BF16),
            cw=conv_w[l].astype(F32), cb=conv_b[l].astype(F32).reshape(1, CONV_WIDTH),
            ln1g=ln1_g[l].astype(F32).reshape(1, D_MODEL),
            ln1b=ln1_b[l].astype(F32).reshape(1, D_MODEL),
            w1=w_mlp1[l].astype(BF16), w2=w_mlp2[l].astype(BF16),
            ln2g=ln2_g[l].astype(F32).reshape(1, D_MODEL),
            ln2b=ln2_b[l].astype(F32).reshape(1, D_MODEL)))

    y_prompt, y_sample = x_prompt, x_sample
    for lw in layers:
        y_prompt = _encoder_layer(y_prompt, lw, bias, tile, alpha)
        y_sample = _encoder_layer(y_sample, lw, bias, tile, alpha)
    return (y_prompt, y_sample)
```

```python
import functools
import math

import jax
import jax.numpy as jnp
from jax import lax
from jax.experimental import pallas as pl
from jax.experimental.pallas import tpu as pltpu

D_MODEL = 1024
HEAD_DIM = 64
V_DIM = 2 * HEAD_DIM
BF16_SUBLANES = 16
V_ROWS = V_DIM + BF16_SUBLANES
N_HEADS = 4
ATTN_WIDTH = N_HEADS * V_DIM
CONV_WIDTH = D_MODEL - ATTN_WIDTH
D_FF = 4 * D_MODEL
NUM_BUCKETS = 32
MAX_DISTANCE = 128
LN_EPS = 1e-5
LOG2_E = math.log2(math.e)
HALO_ROWS = 8
SEQ_TILE = 512
ATTN_TILE_FACTOR = 2
BIAS_BLOCK = 128
BIAS_REACH = 2
FF_CHUNK = 1024
VMEM_LIMIT_BYTES = 56 * 1024 * 1024
NEG_BIG = -1e30

F32 = jnp.float32
BF16 = jnp.bfloat16


def _t5_bucket(rel):
    half = NUM_BUCKETS // 2
    max_exact = half // 2
    ret = jnp.where(rel > 0, half, 0)
    n = jnp.abs(rel)
    nf = jnp.maximum(n, 1).astype(F32)
    large = max_exact + (jnp.log(nf / max_exact) / math.log(MAX_DISTANCE / max_exact)
                         * (half - max_exact)).astype(jnp.int32)
    large = jnp.minimum(large, half - 1)
    return ret + jnp.where(n < max_exact, n, large)


def _toeplitz_kernel(u_ref, o_ref):
    n = o_ref.shape[0]
    x = jnp.broadcast_to(u_ref[...], (n, 2 * n))
    o_ref[...] = pltpu.roll(x, n + 1, 1, stride=1, stride_axis=0)[:, :n]


def _bias_blocks(rel_bias):
    assert BIAS_BLOCK >= MAX_DISTANCE and BIAS_REACH >= 2
    n = BIAS_BLOCK
    n_d = 2 * BIAS_REACH + 1
    span = (BIAS_REACH + 1) * n
    rel = jnp.arange(-span, span, dtype=jnp.int32)
    tab = rel_bias.astype(F32)[_t5_bucket(rel)].T * LOG2_E
    i = jnp.arange(2 * n, dtype=jnp.int32)[None, :]
    d = jnp.arange(-BIAS_REACH, BIAS_REACH + 1, dtype=jnp.int32)[:, None]
    u = tab[:, d * n + n - 1 - i + span][:, :, None, :]
    return pl.pallas_call(
        _toeplitz_kernel,
        grid=(N_HEADS, n_d),
        in_specs=[pl.BlockSpec((None, None, 1, 2 * n), lambda h, dd: (h, dd, 0, 0))],
        out_specs=pl.BlockSpec((None, None, n, n), lambda h, dd: (h, dd, 0, 0)),
        out_shape=jax.ShapeDtypeStruct((N_HEADS, n_d, n, n), F32),
        name="bias_toeplitz",
    )(u)


def _proj_kernel(x_ref, wk_ref, wqvt_ref, wc_ref, k_ref, qt_ref, vt_ref, gb_ref, u_ref):
    xb = x_ref[...].astype(BF16)
    k = jnp.dot(xb, wk_ref[...], preferred_element_type=F32)
    qv_t = lax.dot_general(wqvt_ref[...], xb, (((1,), (1,)), ((), ())),
                           preferred_element_type=F32)
    c = jnp.dot(xb, wc_ref[...], preferred_element_type=F32)
    scale = HEAD_DIM ** -0.5 * LOG2_E
    pad_row = lax.broadcasted_iota(jnp.int32, (V_ROWS - V_DIM, xb.shape[0]), 0)
    ones_row = jnp.where(pad_row == 0, 1.0, 0.0).astype(BF16)
    for h in range(N_HEADS):
        k_ref[h] = k[:, h * V_DIM:(h + 1) * V_DIM].astype(BF16)
        qt_ref[h] = (qv_t[h * V_DIM:(h + 1) * V_DIM, :] * scale).astype(BF16)
        vt_ref[h, :V_DIM] = qv_t[ATTN_WIDTH + h * V_DIM:ATTN_WIDTH + (h + 1) * V_DIM, :].astype(BF16)
        vt_ref[h, V_DIM:] = ones_row
    gb_ref[...] = c[:, :CONV_WIDTH]
    u_ref[...] = c[:, CONV_WIDTH:2 * CONV_WIDTH] * c[:, 2 * CONV_WIDTH:]


def _project(x, wk, wqvt, wc, tile):
    b, s, _ = x.shape
    nt = s // tile
    const = lambda shape: pl.BlockSpec(shape, lambda bi, ti: (0,) * len(shape),
                                       pipeline_mode=pl.Buffered(1))
    head_major = lambda shape: pl.BlockSpec((None, N_HEADS, None) + shape,
                                            lambda bi, ti: (bi, 0, ti, 0, 0))
    rows = lambda width: pl.BlockSpec((None, tile, width), lambda bi, ti: (bi, ti, 0))
    return pl.pallas_call(
        _proj_kernel,
        grid=(b, nt),
        in_specs=[rows(D_MODEL), const(wk.shape), const(wqvt.shape), const(wc.shape)],
        out_specs=[head_major((tile, V_DIM)), head_major((V_DIM, tile)),
                   head_major((V_ROWS, tile)), rows(CONV_WIDTH), rows(CONV_WIDTH)],
        out_shape=[jax.ShapeDtypeStruct((b, N_HEADS, nt, tile, V_DIM), BF16),
                   jax.ShapeDtypeStruct((b, N_HEADS, nt, V_DIM, tile), BF16),
                   jax.ShapeDtypeStruct((b, N_HEADS, nt, V_ROWS, tile), BF16),
                   jax.ShapeDtypeStruct((b, s, CONV_WIDTH), F32),
                   jax.ShapeDtypeStruct((b, s, CONV_WIDTH), F32)],
        compiler_params=pltpu.CompilerParams(
            dimension_semantics=("parallel", "parallel"),
            vmem_limit_bytes=VMEM_LIMIT_BYTES),
        name="in_proj",
    )(x, wk, wqvt, wc)


def _attn_kernel(linit_ref, trips_ref, qt_ref, k_ref, vt_ref, bias_ref, lamv_ref, g_ref,
                 o_ref, qz_ref, s_ref, smax_ref, p_ref, alpha_ref, m_ref, acc_ref,
                 *, n_tiles):
    qi = pl.program_id(2)
    n_sub = qt_ref.shape[0]
    ta = qz_ref.shape[2]
    n_blk = ta // BIAS_BLOCK

    qt = jnp.concatenate([qt_ref[a] for a in range(n_sub)], axis=1).astype(F32)
    row = lax.broadcasted_iota(jnp.int32, qt.shape, 0)
    qz_ref[0] = jnp.where(row < HEAD_DIM, qt, 0.0).astype(BF16)
    qz_ref[1] = jnp.where(row >= HEAD_DIM, qt, 0.0).astype(BF16)
    m_ref[...] = jnp.full(m_ref.shape, NEG_BIG, F32)
    acc_ref[...] = jnp.zeros(acc_ref.shape, F32)

    def bias_tile(ki):
        base = (ki - qi) * n_blk
        blk = {dl: bias_ref[jnp.clip(base + dl, -BIAS_REACH, BIAS_REACH) + BIAS_REACH]
               for dl in range(-(n_blk - 1), n_blk)}
        return jnp.concatenate(
            [jnp.concatenate([blk[i - j] for j in range(n_blk)], axis=1) for i in range(n_blk)],
            axis=0)

    def scores(ki):
        kt = k_ref[ki]
        bias_t = bias_tile(ki)
        for mp in range(2):
            s = jnp.dot(kt, qz_ref[mp], preferred_element_type=F32) + bias_t
            s_ref[mp] = s
            smax_ref[mp] = jnp.max(s, axis=0, keepdims=True)

    def softmax(slot):
        for mp in range(2):
            m_prev = m_ref[mp]
            m_new = jnp.maximum(m_prev, smax_ref[mp])
            p_ref[slot, mp] = jnp.exp2(s_ref[mp] - m_new).astype(BF16)
            alpha_ref[slot, mp] = jnp.exp2(m_prev - m_new)
            m_ref[mp] = m_new

    def pv(ki, slot):
        vt = jnp.concatenate([vt_ref[ki * n_sub + a] for a in range(n_sub)], axis=1)
        for mp in range(2):
            acc_ref[mp] = alpha_ref[slot, mp] * acc_ref[mp] + jnp.dot(
                vt, p_ref[slot, mp], preferred_element_type=F32)

    def two_iterations(jj, carry):
        for u in range(2):
            j = 1 + 2 * jj + u
            softmax((1 + u) % 2)
            scores(j + 1)
            pv(j - 1, u % 2)
        return carry

    scores(0)
    softmax(0)
    scores(1)
    lax.fori_loop(0, trips_ref[0], two_iterations, 0)
    softmax(1)
    pv(n_tiles - 2, 0)
    pv(n_tiles - 1, 1)

    lamv = lamv_ref[...]
    lam = (jnp.exp(jnp.sum(lamv[0:1] * lamv[1:2], axis=1, keepdims=True))
           - jnp.exp(jnp.sum(lamv[2:3] * lamv[3:4], axis=1, keepdims=True))
           + linit_ref[0])
    o = (acc_ref[0, :V_DIM] / acc_ref[0, V_DIM:V_DIM + 1]
         - lam * (acc_ref[1, :V_DIM] / acc_ref[1, V_DIM:V_DIM + 1]))
    o = o * lax.rsqrt(jnp.mean(jnp.square(o), axis=0, keepdims=True) + LN_EPS)
    o = o * g_ref[...] * (1.0 - linit_ref[0])
    o_ref[...] = o.T.astype(o_ref.dtype)


def _attention(linit, q_t, k, v_t, bias, lamv, g_col, seq_len):
    b, _, n_proj_tiles, _, tile = q_t.shape
    n_sub = ATTN_TILE_FACTOR
    ta = n_sub * tile
    assert seq_len % ta == 0 and ta % BIAS_BLOCK == 0
    n_tiles = seq_len // ta
    assert n_tiles % 2 == 0
    k = k.reshape(b, N_HEADS, n_tiles, ta, V_DIM)
    smem = pl.BlockSpec(memory_space=pltpu.SMEM)
    return pl.pallas_call(
        functools.partial(_attn_kernel, n_tiles=n_tiles),
        grid=(b, N_HEADS, n_tiles),
        in_specs=[smem, smem,
                  pl.BlockSpec((None, None, n_sub, V_DIM, tile),
                               lambda bi, hi, qi: (bi, hi, qi, 0, 0)),
                  pl.BlockSpec((None, None, n_tiles, ta, V_DIM),
                               lambda bi, hi, qi: (bi, hi, 0, 0, 0)),
                  pl.BlockSpec((None, None, n_proj_tiles, V_ROWS, tile),
                               lambda bi, hi, qi: (bi, hi, 0, 0, 0)),
                  pl.BlockSpec((None,) + bias.shape[1:], lambda bi, hi, qi: (hi, 0, 0, 0)),
                  pl.BlockSpec(lamv.shape, lambda bi, hi, qi: (0, 0)),
                  pl.BlockSpec(g_col.shape, lambda bi, hi, qi: (0, 0))],
        out_specs=pl.BlockSpec((None, ta, V_DIM), lambda bi, hi, qi: (bi, qi, hi)),
        out_shape=jax.ShapeDtypeStruct((b, seq_len, ATTN_WIDTH), BF16),
        scratch_shapes=[pltpu.VMEM((2, V_DIM, ta), BF16),
                        pltpu.VMEM((2, ta, ta), F32),
                        pltpu.VMEM((2, 1, ta), F32),
                        pltpu.VMEM((2, 2, ta, ta), BF16),
                        pltpu.VMEM((2, 2, 1, ta), F32),
                        pltpu.VMEM((2, 1, ta), F32),
                        pltpu.VMEM((2, V_ROWS, ta), F32)],
        compiler_params=pltpu.CompilerParams(
            dimension_semantics=("parallel", "parallel", "arbitrary"),
            vmem_limit_bytes=VMEM_LIMIT_BYTES),
        name="diff_attn",
    )(linit, jnp.full((1,), (n_tiles - 2) // 2, jnp.int32), q_t, k, v_t, bias, lamv, g_col)


def _layer_norm(y, g, b):
    mu = jnp.mean(y, axis=-1, keepdims=True)
    yc = y - mu
    var = jnp.mean(jnp.square(yc), axis=-1, keepdims=True)
    return yc * lax.rsqrt(var + LN_EPS) * g + b


def _mix_mlp_kernel(attn_ref, gb_ref, u_ref, uprev_ref, unext_ref, x_ref, wout_ref, cw_ref,
                    cb_ref, ln1g_ref, ln1b_ref, w1_ref, w2_ref, ln2g_ref, ln2b_ref, o_ref,
                    *, alpha, n_tiles):
    ti = pl.program_id(1)
    u = u_ref[...]
    tile = u.shape[0]
    prev_row = jnp.where(ti > 0, uprev_ref[HALO_ROWS - 1:HALO_ROWS, :], 0.0)
    next_row = jnp.where(ti < n_tiles - 1, unext_ref[0:1, :], 0.0)
    row = lax.broadcasted_iota(jnp.int32, u.shape, 0)
    u_prev = jnp.where(row == 0, prev_row, pltpu.roll(u, 1, axis=0))
    u_next = jnp.where(row == tile - 1, next_row, pltpu.roll(u, tile - 1, axis=0))
    cw = cw_ref[...]
    conv = cw[0:1] * u_prev + cw[1:2] * u + cw[2:3] * u_next + cb_ref[...]
    conv_out = (gb_ref[...] * conv).astype(BF16)
    mix = (jnp.dot(attn_ref[...], wout_ref[:ATTN_WIDTH, :], preferred_element_type=F32)
           + jnp.dot(conv_out, wout_ref[ATTN_WIDTH:, :], preferred_element_type=F32))
    x1 = _layer_norm(alpha * x_ref[...] + mix, ln1g_ref[...], ln1b_ref[...])
    x1b = x1.astype(BF16)
    ffn = jnp.zeros_like(x1)
    for c0 in range(0, D_FF, FF_CHUNK):
        hid = jnp.dot(x1b, w1_ref[:, c0:c0 + FF_CHUNK], preferred_element_type=F32)
        hid = jnp.square(jnp.maximum(hid, 0.0)).astype(BF16)
        ffn = ffn + jnp.dot(hid, w2_ref[c0:c0 + FF_CHUNK, :], preferred_element_type=F32)
    o_ref[...] = _layer_norm(alpha * x1 + ffn, ln2g_ref[...], ln2b_ref[...])


def _mix_mlp(attn, gb, u, x, wout, cw, cb, ln1g, ln1b, w1, w2, ln2g, ln2b, tile, alpha):
    b, s, _ = x.shape
    nt = s // tile
    halo_per_tile = tile // HALO_ROWS
    n_halo = s // HALO_ROWS
    const = lambda arr: pl.BlockSpec(arr.shape, lambda bi, ti: (0,) * arr.ndim,
                                     pipeline_mode=pl.Buffered(1))
    rows = lambda width: pl.BlockSpec((None, tile, width), lambda bi, ti: (bi, ti, 0))
    return pl.pallas_call(
        functools.partial(_mix_mlp_kernel, alpha=alpha, n_tiles=nt),
        grid=(b, nt),
        in_specs=[rows(ATTN_WIDTH), rows(CONV_WIDTH), rows(CONV_WIDTH),
                  pl.BlockSpec((None, HALO_ROWS, CONV_WIDTH),
                               lambda bi, ti: (bi, jnp.maximum(ti * halo_per_tile - 1, 0), 0)),
                  pl.BlockSpec((None, HALO_ROWS, CONV_WIDTH),
                               lambda bi, ti: (bi, jnp.minimum((ti + 1) * halo_per_tile,
                                                               n_halo - 1), 0)),
                  rows(D_MODEL), const(wout), const(cw), const(cb), const(ln1g), const(ln1b),
                  const(w1), const(w2), const(ln2g), const(ln2b)],
        out_specs=rows(D_MODEL),
        out_shape=jax.ShapeDtypeStruct((b, s, D_MODEL), F32),
        compiler_params=pltpu.CompilerParams(
            dimension_semantics=("parallel", "parallel"),
            vmem_limit_bytes=VMEM_LIMIT_BYTES),
        name="mix_mlp",
    )(attn, gb, u, u, u, x, wout, cw, cb, ln1g, ln1b, w1, w2, ln2g, ln2b)


def _encoder_layer(x, lw, bias, tile, alpha):
    s = x.shape[1]
    k, q_t, v_t, gb, u = _project(x, lw["wk"], lw["wqvt"], lw["wc"], tile)
    attn = _attention(lw["linit"], q_t, k, v_t, bias, lw["lamv"], lw["g_col"], s)
    return _mix_mlp(attn, gb, u, x, lw["wout"], lw["cw"], lw["cb"], lw["ln1g"], lw["ln1b"],
                    lw["w1"], lw["w2"], lw["ln2g"], lw["ln2b"], tile, alpha)


def kernel(x_prompt, x_sample, w_in, w_out, conv_w, conv_b, lambda_q1, lambda_k1, lambda_q2,
           lambda_k2, subln_g, rel_bias, ln1_g, ln1_b, w_mlp1, w_mlp2, ln2_g, ln2_b):
    depth = w_in.shape[0]
    alpha = (2.0 * depth) ** 0.25
    tile = SEQ_TILE
    for x in (x_prompt, x_sample):
        assert x.shape[1] % (ATTN_TILE_FACTOR * tile) == 0 and x.shape[2] == D_MODEL
    bias = _bias_blocks(rel_bias)

    qk = 2 * ATTN_WIDTH
    layers = []
    for l in range(depth):
        wl = w_in[l]
        linit = 0.8 - 0.6 * math.exp(-0.3 * l)
        layers.append(dict(
            wk=wl[:, ATTN_WIDTH:qk].astype(BF16),
            wqvt=jnp.concatenate([wl[:, :ATTN_WIDTH], wl[:, qk:qk + ATTN_WIDTH]],
                                 axis=1).T.astype(BF16),
            wc=wl[:, qk + ATTN_WIDTH:].astype(BF16),
            linit=jnp.full((1,), linit, F32),
            lamv=jnp.stack([lambda_q1[l], lambda_k1[l], lambda_q2[l], lambda_k2[l]]).astype(F32),
            g_col=subln_g[l].astype(F32).reshape(V_DIM, 1),
            wout=w_out[l].astype(BF16),
            cw=conv_w[l].astype(F32), cb=conv_b[l].astype(F32).reshape(1, CONV_WIDTH),
            ln1g=ln1_g[l].astype(F32).reshape(1, D_MODEL),
            ln1b=ln1_b[l].astype(F32).reshape(1, D_MODEL),
            w1=w_mlp1[l].astype(BF16), w2=w_mlp2[l].astype(BF16),
            ln2g=ln2_g[l].astype(F32).reshape(1, D_MODEL),
            ln2b=ln2_b[l].astype(F32).reshape(1, D_MODEL)))

    y_prompt, y_sample = x_prompt, x_sample
    for lw in layers:
        y_prompt = _encoder_layer(y_prompt, lw, bias, tile, alpha)
        y_sample = _encoder_layer(y_sample, lw, bias, tile, alpha)
    return (y_prompt, y_sample)
```

```python
import functools
import math

import jax
import jax.numpy as jnp
from jax import lax
from jax.experimental import pallas as pl
from jax.experimental.pallas import tpu as pltpu

D_MODEL = 1024
HEAD_DIM = 64
V_DIM = 2 * HEAD_DIM
BF16_SUBLANES = 16
V_ROWS = V_DIM + BF16_SUBLANES
N_HEADS = 4
ATTN_WIDTH = N_HEADS * V_DIM
CONV_WIDTH = D_MODEL - ATTN_WIDTH
D_FF = 4 * D_MODEL
NUM_BUCKETS = 32
MAX_DISTANCE = 128
LN_EPS = 1e-5
LOG2_E = math.log2(math.e)
HALO_ROWS = 8
SEQ_TILE = 512
ATTN_TILE_FACTOR = 2
BIAS_BLOCK = 128
BIAS_REACH = 2
FF_CHUNK = 1024
VMEM_LIMIT_BYTES = 56 * 1024 * 1024
NEG_BIG = -1e30

F32 = jnp.float32
BF16 = jnp.bfloat16


def _t5_bucket(rel):
    half = NUM_BUCKETS // 2
    max_exact = half // 2
    ret = jnp.where(rel > 0, half, 0)
    n = jnp.abs(rel)
    nf = jnp.maximum(n, 1).astype(F32)
    large = max_exact + (jnp.log(nf / max_exact) / math.log(MAX_DISTANCE / max_exact)
                         * (half - max_exact)).astype(jnp.int32)
    large = jnp.minimum(large, half - 1)
    return ret + jnp.where(n < max_exact, n, large)


def _toeplitz_kernel(u_ref, o_ref):
    n = o_ref.shape[0]
    x = jnp.broadcast_to(u_ref[...], (n, 2 * n))
    o_ref[...] = pltpu.roll(x, n + 1, 1, stride=1, stride_axis=0)[:, :n]


def _bias_blocks(rel_bias):
    assert BIAS_BLOCK >= MAX_DISTANCE and BIAS_REACH >= 2
    n = BIAS_BLOCK
    n_d = 2 * BIAS_REACH + 1
    span = (BIAS_REACH + 1) * n
    rel = jnp.arange(-span, span, dtype=jnp.int32)
    tab = rel_bias.astype(F32)[_t5_bucket(rel)].T * LOG2_E
    i = jnp.arange(2 * n, dtype=jnp.int32)[None, :]
    d = jnp.arange(-BIAS_REACH, BIAS_REACH + 1, dtype=jnp.int32)[:, None]
    u = tab[:, d * n + n - 1 - i + span][:, :, None, :]
    return pl.pallas_call(
        _toeplitz_kernel,
        grid=(N_HEADS, n_d),
        in_specs=[pl.BlockSpec((None, None, 1, 2 * n), lambda h, dd: (h, dd, 0, 0))],
        out_specs=pl.BlockSpec((None, None, n, n), lambda h, dd: (h, dd, 0, 0)),
        out_shape=jax.ShapeDtypeStruct((N_HEADS, n_d, n, n), F32),
        name="bias_toeplitz",
    )(u)


def _proj_kernel(x_ref, wk_ref, wqvt_ref, wc_ref, k_ref, qt_ref, vt_ref, gb_ref, u_ref):
    xb = x_ref[...].astype(BF16)
    k = jnp.dot(xb, wk_ref[...], preferred_element_type=F32)
    qv_t = lax.dot_general(wqvt_ref[...], xb, (((1,), (1,)), ((), ())),
                           preferred_element_type=F32)
    c = jnp.dot(xb, wc_ref[...], preferred_element_type=F32)
    scale = HEAD_DIM ** -0.5 * LOG2_E
    pad_row = lax.broadcasted_iota(jnp.int32, (V_ROWS - V_DIM, xb.shape[0]), 0)
    ones_row = jnp.where(pad_row == 0, 1.0, 0.0).astype(BF16)
    for h in range(N_HEADS):
        k_ref[h] = k[:, h * V_DIM:(h + 1) * V_DIM].astype(BF16)
        qt_ref[h] = (qv_t[h * V_DIM:(h + 1) * V_DIM, :] * scale).astype(BF16)
        vt_ref[h, :V_DIM] = qv_t[ATTN_WIDTH + h * V_DIM:ATTN_WIDTH + (h + 1) * V_DIM, :].astype(BF16)
        vt_ref[h, V_DIM:] = ones_row
    gb_ref[...] = c[:, :CONV_WIDTH]
    u_ref[...] = c[:, CONV_WIDTH:2 * CONV_WIDTH] * c[:, 2 * CONV_WIDTH:]


def _project(x, wk, wqvt, wc, tile):
    b, s, _ = x.shape
    nt = s // tile
    const = lambda shape: pl.BlockSpec(shape, lambda bi, ti: (0,) * len(shape),
                                       pipeline_mode=pl.Buffered(1))
    head_major = lambda shape: pl.BlockSpec((None, N_HEADS, None) + shape,
                                            lambda bi, ti: (bi, 0, ti, 0, 0))
    rows = lambda width: pl.BlockSpec((None, tile, width), lambda bi, ti: (bi, ti, 0))
    return pl.pallas_call(
        _proj_kernel,
        grid=(b, nt),
        in_specs=[rows(D_MODEL), const(wk.shape), const(wqvt.shape), const(wc.shape)],
        out_specs=[head_major((tile, V_DIM)), head_major((V_DIM, tile)),
                   head_major((V_ROWS, tile)), rows(CONV_WIDTH), rows(CONV_WIDTH)],
        out_shape=[jax.ShapeDtypeStruct((b, N_HEADS, nt, tile, V_DIM), BF16),
                   jax.ShapeDtypeStruct((b, N_HEADS, nt, V_DIM, tile), BF16),
                   jax.ShapeDtypeStruct((b, N_HEADS, nt, V_ROWS, tile), BF16),
                   jax.ShapeDtypeStruct((b, s, CONV_WIDTH), F32),
                   jax.ShapeDtypeStruct((b, s, CONV_WIDTH), F32)],
        compiler_params=pltpu.CompilerParams(
            dimension_semantics=("parallel", "parallel"),
            vmem_limit_bytes=VMEM_LIMIT_BYTES),
        name="in_proj",
    )(x, wk, wqvt, wc)


def _attn_kernel(linit_ref, trips_ref, qt_ref, k_ref, vt_ref, bias_ref, lamv_ref, g_ref,
                 o_ref, qz_ref, s_ref, smax_ref, p_ref, alpha_ref, m_ref, acc_ref,
                 *, n_tiles):
    qi = pl.program_id(2)
    n_sub = qt_ref.shape[0]
    ta = qz_ref.shape[2]
    n_blk = ta // BIAS_BLOCK

    qt = jnp.concatenate([qt_ref[a] for a in range(n_sub)], axis=1).astype(F32)
    row = lax.broadcasted_iota(jnp.int32, qt.shape, 0)
    qz_ref[0] = jnp.where(row < HEAD_DIM, qt, 0.0).astype(BF16)
    qz_ref[1] = jnp.where(row >= HEAD_DIM, qt, 0.0).astype(BF16)
    m_ref[...] = jnp.full(m_ref.shape, NEG_BIG, F32)
    acc_ref[...] = jnp.zeros(acc_ref.shape, F32)

    def bias_tile(ki):
        base = (ki - qi) * n_blk
        blk = {dl: bias_ref[jnp.clip(base + dl, -BIAS_REACH, BIAS_REACH) + BIAS_REACH]
               for dl in range(-(n_blk - 1), n_blk)}
        return jnp.concatenate(
            [jnp.concatenate([blk[i - j] for j in range(n_blk)], axis=1) for i in range(n_blk)],
            axis=0)

    def scores(ki, buf):
        kt = k_ref[ki]
        bias_t = bias_tile(ki)
        for mp in range(2):
            s = jnp.dot(kt, qz_ref[mp], preferred_element_type=F32) + bias_t
            s_ref[buf, mp] = s
            smax_ref[buf, mp] = jnp.max(s, axis=0, keepdims=True)

    def softmax(slot):
        for mp in range(2):
            m_prev = m_ref[mp]
            m_new = jnp.maximum(m_prev, smax_ref[slot, mp])
            p_ref[slot, mp] = jnp.exp2(s_ref[slot, mp] - m_new).astype(BF16)
            alpha_ref[slot, mp] = jnp.exp2(m_prev - m_new)
            m_ref[mp] = m_new

    def pv(ki, slot):
        vt = jnp.concatenate([vt_ref[ki * n_sub + a] for a in range(n_sub)], axis=1)
        for mp in range(2):
            acc_ref[mp] = alpha_ref[slot, mp] * acc_ref[mp] + jnp.dot(
                vt, p_ref[slot, mp], preferred_element_type=F32)

    def two_iterations(jj, carry):
        for u in range(2):
            j = 1 + 2 * jj + u
            softmax((1 + u) % 2)
            scores(j + 1, u % 2)
            pv(j - 1, u % 2)
        return carry

    scores(0, 0)
    softmax(0)
    scores(1, 1)
    lax.fori_loop(0, trips_ref[0], two_iterations, 0)
    softmax(1)
    pv(n_tiles - 2, 0)
    pv(n_tiles - 1, 1)

    lamv = lamv_ref[...]
    lam = (jnp.exp(jnp.sum(lamv[0:1] * lamv[1:2], axis=1, keepdims=True))
           - jnp.exp(jnp.sum(lamv[2:3] * lamv[3:4], axis=1, keepdims=True))
           + linit_ref[0])
    o = (acc_ref[0, :V_DIM] / acc_ref[0, V_DIM:V_DIM + 1]
         - lam * (acc_ref[1, :V_DIM] / acc_ref[1, V_DIM:V_DIM + 1]))
    o = o * lax.rsqrt(jnp.mean(jnp.square(o), axis=0, keepdims=True) + LN_EPS)
    o = o * g_ref[...] * (1.0 - linit_ref[0])
    o_ref[...] = o.T.astype(o_ref.dtype)


def _attention(linit, q_t, k, v_t, bias, lamv, g_col, seq_len):
    b, _, n_proj_tiles, _, tile = q_t.shape
    n_sub = ATTN_TILE_FACTOR
    ta = n_sub * tile
    assert seq_len % ta == 0 and ta % BIAS_BLOCK == 0
    n_tiles = seq_len // ta
    assert n_tiles % 2 == 0
    k = k.reshape(b, N_HEADS, n_tiles, ta, V_DIM)
    smem = pl.BlockSpec(memory_space=pltpu.SMEM)
    return pl.pallas_call(
        functools.partial(_attn_kernel, n_tiles=n_tiles),
        grid=(b, N_HEADS, n_tiles),
        in_specs=[smem, smem,
                  pl.BlockSpec((None, None, n_sub, V_DIM, tile),
                               lambda bi, hi, qi: (bi, hi, qi, 0, 0)),
                  pl.BlockSpec((None, None, n_tiles, ta, V_DIM),
                               lambda bi, hi, qi: (bi, hi, 0, 0, 0)),
                  pl.BlockSpec((None, None, n_proj_tiles, V_ROWS, tile),
                               lambda bi, hi, qi: (bi, hi, 0, 0, 0)),
                  pl.BlockSpec((None,) + bias.shape[1:], lambda bi, hi, qi: (hi, 0, 0, 0)),
                  pl.BlockSpec(lamv.shape, lambda bi, hi, qi: (0, 0)),
                  pl.BlockSpec(g_col.shape, lambda bi, hi, qi: (0, 0))],
        out_specs=pl.BlockSpec((None, ta, V_DIM), lambda bi, hi, qi: (bi, qi, hi)),
        out_shape=jax.ShapeDtypeStruct((b, seq_len, ATTN_WIDTH), BF16),
        scratch_shapes=[pltpu.VMEM((2, V_DIM, ta), BF16),
                        pltpu.VMEM((2, 2, ta, ta), F32),
                        pltpu.VMEM((2, 2, 1, ta), F32),
                        pltpu.VMEM((2, 2, ta, ta), BF16),
                        pltpu.VMEM((2, 2, 1, ta), F32),
                        pltpu.VMEM((2, 1, ta), F32),
                        pltpu.VMEM((2, V_ROWS, ta), F32)],
        compiler_params=pltpu.CompilerParams(
            dimension_semantics=("parallel", "parallel", "arbitrary"),
            vmem_limit_bytes=VMEM_LIMIT_BYTES),
        name="diff_attn",
    )(linit, jnp.full((1,), (n_tiles - 2) // 2, jnp.int32), q_t, k, v_t, bias, lamv, g_col)


def _layer_norm(y, g, b):
    mu = jnp.mean(y, axis=-1, keepdims=True)
    yc = y - mu
    var = jnp.mean(jnp.square(yc), axis=-1, keepdims=True)
    return yc * lax.rsqrt(var + LN_EPS) * g + b


def _mix_mlp_kernel(attn_ref, gb_ref, u_ref, uprev_ref, unext_ref, x_ref, wout_ref, cw_ref,
                    cb_ref, ln1g_ref, ln1b_ref, w1_ref, w2_ref, ln2g_ref, ln2b_ref, o_ref,
                    *, alpha, n_tiles):
    ti = pl.program_id(1)
    u = u_ref[...]
    tile = u.shape[0]
    prev_row = jnp.where(ti > 0, uprev_ref[HALO_ROWS - 1:HALO_ROWS, :], 0.0)
    next_row = jnp.where(ti < n_tiles - 1, unext_ref[0:1, :], 0.0)
    row = lax.broadcasted_iota(jnp.int32, u.shape, 0)
    u_prev = jnp.where(row == 0, prev_row, pltpu.roll(u, 1, axis=0))
    u_next = jnp.where(row == tile - 1, next_row, pltpu.roll(u, tile - 1, axis=0))
    cw = cw_ref[...]
    conv = cw[0:1] * u_prev + cw[1:2] * u + cw[2:3] * u_next + cb_ref[...]
    conv_out = (gb_ref[...] * conv).astype(BF16)
    mix = (jnp.dot(attn_ref[...], wout_ref[:ATTN_WIDTH, :], preferred_element_type=F32)
           + jnp.dot(conv_out, wout_ref[ATTN_WIDTH:, :], preferred_element_type=F32))
    x1 = _layer_norm(alpha * x_ref[...] + mix, ln1g_ref[...], ln1b_ref[...])
    x1b = x1.astype(BF16)
    ffn = jnp.zeros_like(x1)
    for c0 in range(0, D_FF, FF_CHUNK):
        hid = jnp.dot(x1b, w1_ref[:, c0:c0 + FF_CHUNK], preferred_element_type=F32)
        hid = jnp.square(jnp.maximum(hid, 0.0)).astype(BF16)
        ffn = ffn + jnp.dot(hid, w2_ref[c0:c0 + FF_CHUNK, :], preferred_element_type=F32)
    o_ref[...] = _layer_norm(alpha * x1 + ffn, ln2g_ref[...], ln2b_ref[...])


def _mix_mlp(attn, gb, u, x, wout, cw, cb, ln1g, ln1b, w1, w2, ln2g, ln2b, tile, alpha):
    b, s, _ = x.shape
    nt = s // tile
    halo_per_tile = tile // HALO_ROWS
    n_halo = s // HALO_ROWS
    const = lambda arr: pl.BlockSpec(arr.shape, lambda bi, ti: (0,) * arr.ndim,
                                     pipeline_mode=pl.Buffered(1))
    rows = lambda width: pl.BlockSpec((None, tile, width), lambda bi, ti: (bi, ti, 0))
    return pl.pallas_call(
        functools.partial(_mix_mlp_kernel, alpha=alpha, n_tiles=nt),
        grid=(b, nt),
        in_specs=[rows(ATTN_WIDTH), rows(CONV_WIDTH), rows(CONV_WIDTH),
                  pl.BlockSpec((None, HALO_ROWS, CONV_WIDTH),
                               lambda bi, ti: (bi, jnp.maximum(ti * halo_per_tile - 1, 0), 0)),
                  pl.BlockSpec((None, HALO_ROWS, CONV_WIDTH),
                               lambda bi, ti: (bi, jnp.minimum((ti + 1) * halo_per_tile,
                                                               n_halo - 1), 0)),
                  rows(D_MODEL), const(wout), const(cw), const(cb), const(ln1g), const(ln1b),
                  const(w1), const(w2), const(ln2g), const(ln2b)],
        out_specs=rows(D_MODEL),
        out_shape=jax.ShapeDtypeStruct((b, s, D_MODEL), F32),
        compiler_params=pltpu.CompilerParams(
            dimension_semantics=("parallel", "parallel"),
            vmem_limit_bytes=VMEM_LIMIT_BYTES),
        name="mix_mlp",
    )(attn, gb, u, u, u, x, wout, cw, cb, ln1g, ln1b, w1, w2, ln2g, ln2b)


def _encoder_layer(x, lw, bias, tile, alpha):
    s = x.shape[1]
    k, q_t, v_t, gb, u = _project(x, lw["wk"], lw["wqvt"], lw["wc"], tile)
    attn = _attention(lw["linit"], q_t, k, v_t, bias, lw["lamv"], lw["g_col"], s)
    return _mix_mlp(attn, gb, u, x, lw["wout"], lw["cw"], lw["cb"], lw["ln1g"], lw["ln1b"],
                    lw["w1"], lw["w2"], lw["ln2g"], lw["ln2b"], tile, alpha)


def kernel(x_prompt, x_sample, w_in, w_out, conv_w, conv_b, lambda_q1, lambda_k1, lambda_q2,
           lambda_k2, subln_g, rel_bias, ln1_g, ln1_b, w_mlp1, w_mlp2, ln2_g, ln2_b):
    depth = w_in.shape[0]
    alpha = (2.0 * depth) ** 0.25
    tile = SEQ_TILE
    for x in (x_prompt, x_sample):
        assert x.shape[1] % (ATTN_TILE_FACTOR * tile) == 0 and x.shape[2] == D_MODEL
    bias = _bias_blocks(rel_bias)

    qk = 2 * ATTN_WIDTH
    layers = []
    for l in range(depth):
        wl = w_in[l]
        linit = 0.8 - 0.6 * math.exp(-0.3 * l)
        layers.append(dict(
            wk=wl[:, ATTN_WIDTH:qk].astype(BF16),
            wqvt=jnp.concatenate([wl[:, :ATTN_WIDTH], wl[:, qk:qk + ATTN_WIDTH]],
                                 axis=1).T.astype(BF16),
            wc=wl[:, qk + ATTN_WIDTH:].astype(BF16),
            linit=jnp.full((1,), linit, F32),
            lamv=jnp.stack([lambda_q1[l], lambda_k1[l], lambda_q2[l], lambda_k2[l]]).astype(F32),
            g_col=subln_g[l].astype(F32).reshape(V_DIM, 1),
            wout=w_out[l].astype(BF16),
            cw=conv_w[l].astype(F32), cb=conv_b[l].astype(F32).reshape(1, CONV_WIDTH),
            ln1g=ln1_g[l].astype(F32).reshape(1, D_MODEL),
            ln1b=ln1_b[l].astype(F32).reshape(1, D_MODEL),
            w1=w_mlp1[l].astype(BF16), w2=w_mlp2[l].astype(BF16),
            ln2g=ln2_g[l].astype(F32).reshape(1, D_MODEL),
            ln2b=ln2_b[l].astype(F32).reshape(1, D_MODEL)))

    y_prompt, y_sample = x_prompt, x_sample
    for lw in layers:
        y_prompt = _encoder_layer(y_prompt, lw, bias, tile, alpha)
        y_sample = _encoder_layer(y_sample, lw, bias, tile, alpha)
    return (y_prompt, y_sample)
```
